```python
import math
import jax, jax.numpy as jnp
from jax import lax
import numpy as np

D_MODEL = 1024
BATCH = 2
SEQ = 8192
DEPTH = 1

POOL_WINDOWS = (2, 4, 8, 16)
N_POOL_GROUPS = 4
POOL_WIDTH = D_MODEL // 2
POOL_GROUP = POOL_WIDTH // N_POOL_GROUPS
POOL_OUT_GROUP = D_MODEL // N_POOL_GROUPS
DN_HEADS = 8
DN_HEAD_DIM = 128
DN_WIDTH = DN_HEADS * DN_HEAD_DIM
CONV_K = 4
CHUNK = 64
D_FF = 4 * D_MODEL
PLE_DIM = 256
LN_EPS = 1e-5
RMS_EPS = 1e-6
L2_EPS = 1e-6
DEEPNORM_ALPHA = (2.0 * DEPTH) ** 0.25
DEEPNORM_BETA = (8.0 * DEPTH) ** -0.25
QKV_WIDTH = 3 * DN_WIDTH
IN_WIDTH = POOL_WIDTH + QKV_WIDTH + DN_WIDTH + DN_HEADS + DN_HEADS + 2 * D_MODEL
SPLIT_POINTS = (
    POOL_WIDTH,
    POOL_WIDTH + QKV_WIDTH,
    POOL_WIDTH + QKV_WIDTH + DN_WIDTH,
    POOL_WIDTH + QKV_WIDTH + DN_WIDTH + DN_HEADS,
    POOL_WIDTH + QKV_WIDTH + DN_WIDTH + 2 * DN_HEADS,
    POOL_WIDTH + QKV_WIDTH + DN_WIDTH + 2 * DN_HEADS + D_MODEL,
)

kernel_name = "hybrid_pool_gdn_deepnorm_ple"


def _layer_norm(x, g, b):
    xf = x.astype(jnp.float32)
    mu = jnp.mean(xf, axis=-1, keepdims=True)
    var = jnp.mean(jnp.square(xf - mu), axis=-1, keepdims=True)
    y = (xf - mu) * lax.rsqrt(var + LN_EPS) * g.astype(jnp.float32) + b.astype(jnp.float32)
    return y.astype(x.dtype)


def _multiscale_pool(u, pool_w, pool_scale):
    bsz, seq, _ = u.shape
    ug = u.astype(jnp.float32).reshape(bsz, seq, N_POOL_GROUPS, POOL_GROUP)
    cs = jnp.cumsum(ug, axis=1)
    outs = []
    for gi, w in enumerate(POOL_WINDOWS):
        c = cs[:, :, gi]
        prev = jnp.pad(c, ((0, 0), (w, 0), (0, 0)))[:, :seq]
        cnt = jnp.minimum(jnp.arange(1, seq + 1), w).astype(jnp.float32)[None, :, None]
        outs.append((c - prev) / cnt - ug[:, :, gi])
    d = jnp.stack(outs, axis=2).astype(u.dtype)
    y = jnp.einsum('bsgc,gcd->bsgd', d, pool_w).reshape(bsz, seq, D_MODEL)
    return y * pool_scale


def _causal_depthwise_conv_silu(x, w):
    ch = x.shape[-1]
    y = lax.conv_general_dilated(
        x, w[:, None, :].astype(x.dtype), window_strides=(1,),
        padding=((CONV_K - 1, 0),), dimension_numbers=('NWC', 'WIO', 'NWC'),
        feature_group_count=ch)
    return jax.nn.silu(y)


def _l2norm(t):
    return t * lax.rsqrt(jnp.sum(jnp.square(t), axis=-1, keepdims=True) + L2_EPS)


def _chunk_gated_delta_rule(q, k, v, beta, g):
    bsz, seq, nh, dk = q.shape
    dv = v.shape[-1]
    n_chunks = seq // CHUNK

    def to_chunks(t):
        t = t.reshape((bsz, n_chunks, CHUNK, nh) + t.shape[3:])
        return jnp.moveaxis(t, 3, 2)

    q, k, v, beta, g = (to_chunks(t) for t in (q, k, v, beta, g))
    gc = jnp.cumsum(g, axis=-1)
    idx = jnp.arange(CHUNK)
    incl = idx[:, None] >= idx[None, :]
    strict = idx[:, None] > idx[None, :]
    decay = jnp.exp(jnp.where(incl, gc[..., :, None] - gc[..., None, :], -jnp.inf))
    kb = k * beta[..., None]
    m = jnp.where(strict, jnp.einsum('bnhid,bnhjd->bnhij', kb, k) * decay, 0.0)
    a = m + jnp.eye(CHUNK, dtype=m.dtype)
    rhs = jnp.concatenate([v * beta[..., None], kb * jnp.exp(gc)[..., None]], axis=-1)
    sol = lax.linalg.triangular_solve(a, rhs, left_side=True, lower=True, unit_diagonal=True)
    u, w = sol[..., :dv], sol[..., dv:]
    attn = jnp.einsum('bnhid,bnhjd->bnhij', q, k) * decay
    qg = q * jnp.exp(gc)[..., None]
    gl = gc[..., -1]
    kg = k * jnp.exp(gl[..., None] - gc)[..., None]
    xs = tuple(jnp.moveaxis(t, 1, 0) for t in (qg, kg, u, w, attn, gl))

    def step(state, inp):
        qg_n, kg_n, u_n, w_n, attn_n, gl_n = inp
        v_new = u_n - jnp.einsum('bhcd,bhde->bhce', w_n, state)
        o = jnp.einsum('bhcd,bhde->bhce', qg_n, state) + jnp.einsum('bhij,bhje->bhie', attn_n, v_new)
        state = state * jnp.exp(gl_n)[..., None, None] + jnp.einsum('bhcd,bhce->bhde', kg_n, v_new)
        return state, o

    s0 = jnp.zeros((bsz, nh, dk, dv), jnp.float32)
    _, o = lax.scan(step, s0, xs)
    o = jnp.moveaxis(o, 0, 1)
    return jnp.moveaxis(o, 2, 3).reshape(bsz, seq, nh, dv)


def _gated_deltanet(qkv, z, beta_raw, a_raw, conv_w, a_log, dt_bias, o_norm_w):
    bsz, seq, _ = qkv.shape
    qkv = _causal_depthwise_conv_silu(qkv, conv_w)
    q, k, v = jnp.split(qkv.astype(jnp.float32), 3, axis=-1)
    shp = (bsz, seq, DN_HEADS, DN_HEAD_DIM)
    q = _l2norm(q.reshape(shp)) * (DN_HEAD_DIM ** -0.5)
    k = _l2norm(k.reshape(shp))
    v = v.reshape(shp)
    beta = jax.nn.sigmoid(beta_raw.astype(jnp.float32))
    g = -jnp.exp(a_log.astype(jnp.float32)) * jax.nn.softplus(
        a_raw.astype(jnp.float32) + dt_bias.astype(jnp.float32))
    o = _chunk_gated_delta_rule(q, k, v, beta, g)
    o = o * lax.rsqrt(jnp.mean(jnp.square(o), axis=-1, keepdims=True) + RMS_EPS)
    o = o * o_norm_w.astype(jnp.float32) * jax.nn.silu(z.astype(jnp.float32).reshape(shp))
    return o.reshape(bsz, seq, DN_WIDTH).astype(qkv.dtype)


def setup_inputs(seed: int = 0) -> dict:
    key = jax.random.key(seed)
    ks = jax.random.split(key, 24)
    f32 = jnp.float32
    nrm = lambda k, shape, s: jax.random.normal(k, shape, f32) * s
    x = jax.random.normal(ks[0], (BATCH, SEQ, D_MODEL), f32)
    p = jax.random.normal(ks[1], (DEPTH, BATCH, SEQ, PLE_DIM), f32)
    ln_in_g = 1.0 + nrm(ks[2], (D_MODEL,), 0.02)
    ln_in_b = nrm(ks[3], (D_MODEL,), 0.02)
    s_in = D_MODEL ** -0.5
    w_pool_qk = nrm(ks[4], (DEPTH, D_MODEL, POOL_WIDTH + 2 * DN_WIDTH), s_in)
    w_v = nrm(ks[5], (DEPTH, D_MODEL, DN_WIDTH), s_in * DEEPNORM_BETA)
    w_rest = nrm(ks[6], (DEPTH, D_MODEL, IN_WIDTH - POOL_WIDTH - QKV_WIDTH), s_in)
    w_in = jnp.concatenate([w_pool_qk, w_v, w_rest], axis=-1)
    pool_w = nrm(ks[7], (DEPTH, N_POOL_GROUPS, POOL_GROUP, POOL_OUT_GROUP), POOL_GROUP ** -0.5)
    pool_scale = 1.0 + nrm(ks[8], (DEPTH, D_MODEL), 0.02)
    conv_w = nrm(ks[9], (DEPTH, CONV_K, QKV_WIDTH), CONV_K ** -0.5)
    a_log = jnp.log(jax.random.uniform(ks[10], (DEPTH, DN_HEADS), f32, 1.0, 16.0))
    dt = jnp.exp(jax.random.uniform(ks[11], (DEPTH, DN_HEADS), f32, math.log(1e-3), math.log(1e-1)))
    dt_bias = dt + jnp.log(-jnp.expm1(-dt))
    o_norm_w = 1.0 + nrm(ks[12], (DEPTH, DN_HEAD_DIM), 0.02)
    w_out = nrm(ks[13], (DEPTH, D_MODEL, D_MODEL), s_in * DEEPNORM_BETA)
    ln1_g = 1.0 + nrm(ks[14], (DEPTH, D_MODEL), 0.02)
    ln1_b = nrm(ks[15], (DEPTH, D_MODEL), 0.02)
    w_up = nrm(ks[16], (DEPTH, D_MODEL, D_FF), s_in)
    w_down = nrm(ks[17], (DEPTH, D_FF, D_MODEL), D_FF ** -0.5 * DEEPNORM_BETA)
    ple_gate_w = nrm(ks[18], (DEPTH, D_MODEL, D_MODEL), s_in)
    ple_proj_w = nrm(ks[19], (DEPTH, PLE_DIM, D_MODEL), PLE_DIM ** -0.5 * DEEPNORM_BETA)
    ln2_g = 1.0 + nrm(ks[20], (DEPTH, D_MODEL), 0.02)
    ln2_b = nrm(ks[21], (DEPTH, D_MODEL), 0.02)
    return {"x": x, "p": p, "ln_in_g": ln_in_g, "ln_in_b": ln_in_b, "w_in": w_in,
            "pool_w": pool_w, "pool_scale": pool_scale, "conv_w": conv_w, "a_log": a_log,
            "dt_bias": dt_bias, "o_norm_w": o_norm_w, "w_out": w_out, "ln1_g": ln1_g,
            "ln1_b": ln1_b, "w_up": w_up, "w_down": w_down, "ple_gate_w": ple_gate_w,
            "ple_proj_w": ple_proj_w, "ln2_g": ln2_g, "ln2_b": ln2_b}


def reference(x, p, ln_in_g, ln_in_b, w_in, pool_w, pool_scale, conv_w, a_log, dt_bias,
              o_norm_w, w_out, ln1_g, ln1_b, w_up, w_down, ple_gate_w, ple_proj_w, ln2_g, ln2_b):
    h = _layer_norm(x, ln_in_g, ln_in_b)
    for i in range(DEPTH):
        proj = h @ w_in[i]
        pool_in, qkv, z, beta_raw, a_raw, gate_a, gate_b = jnp.split(proj, SPLIT_POINTS, axis=-1)
        y_a = _multiscale_pool(pool_in, pool_w[i], pool_scale[i])
        y_b = _gated_deltanet(qkv, z, beta_raw, a_raw, conv_w[i], a_log[i], dt_bias[i], o_norm_w[i])
        mixed = jax.nn.sigmoid(gate_a) * y_a + jax.nn.sigmoid(gate_b) * y_b
        h = _layer_norm(DEEPNORM_ALPHA * h + mixed @ w_out[i], ln1_g[i], ln1_b[i])
        mlp = jnp.square(jax.nn.relu(h @ w_up[i])) @ w_down[i]
        r = DEEPNORM_ALPHA * h + mlp
        ple = jax.nn.sigmoid(r @ ple_gate_w[i]) * (p[i] @ ple_proj_w[i])
        h = _layer_norm(r + ple, ln2_g[i], ln2_b[i])
    return h
```

```python
import functools

import jax
import jax.numpy as jnp
from jax import lax
from jax.experimental import pallas as pl
from jax.experimental.pallas import tpu as pltpu

F32 = jnp.float32
BF16 = jnp.bfloat16

D_MODEL = 1024
POOL_WINDOWS = (2, 4, 8, 16)
N_POOL_GROUPS = 4
POOL_WIDTH = D_MODEL // 2
POOL_GROUP = POOL_WIDTH // N_POOL_GROUPS
POOL_OUT_GROUP = D_MODEL // N_POOL_GROUPS
DN_HEADS = 8
DN_HEAD_DIM = 128
DN_WIDTH = DN_HEADS * DN_HEAD_DIM
CONV_K = 4
D_FF = 4 * D_MODEL
PLE_DIM = 256
LN_EPS = 1e-5
RMS_EPS = 1e-6
L2_EPS = 1e-6
QKV_WIDTH = 3 * DN_WIDTH

LANES = 128
GATE_PAD = LANES
POOL_HIST = 16
CONV_HIST = 8

CHUNK = 128
TILE_PROJ = 256
TILE_MIX = 256
TILE_FFN = 512
VMEM_LIMIT = 56 * 1024 * 1024


def _layer_norm(x, g, b):
    mu = jnp.mean(x, axis=-1, keepdims=True)
    xc = x - mu
    var = jnp.mean(xc * xc, axis=-1, keepdims=True)
    return xc * lax.rsqrt(var + LN_EPS) * g + b


def _sigmoid(x):
    return 1.0 / (1.0 + jnp.exp(-x))


def _softplus(x):
    return jnp.maximum(x, 0.0) + jnp.log1p(jnp.exp(-jnp.abs(x)))


def _mm(a, b):
    return jnp.dot(a.astype(BF16), b.astype(BF16), preferred_element_type=F32)


def _mm_nt(a, b):
    return lax.dot_general(a.astype(BF16), b.astype(BF16), (((1,), (1,)), ((), ())),
                           preferred_element_type=F32)


def _mm_tn(a, b):
    return lax.dot_general(a.astype(BF16), b.astype(BF16), (((0,), (0,)), ((), ())),
                           preferred_element_type=F32)


def _resident(shape):
    return pl.BlockSpec(shape, lambda *_: (0,) * len(shape), pipeline_mode=pl.Buffered(1))


def _proj_body(x_ref, g_ref, b_ref, w_ref, *out_refs):
    h = _layer_norm(x_ref[...], g_ref[...], b_ref[...]).astype(BF16)
    off = 0
    for ref in out_refs:
        n = ref.shape[-1]
        for c0 in range(0, n, 1024):
            c1 = min(n, c0 + 1024)
            ref[:, c0:c1] = jnp.dot(h, w_ref[:, off + c0:off + c1], preferred_element_type=F32)
        off += n


def _in_projection(x2, ln_g, ln_b, w_cat):
    t = x2.shape[0]
    widths = (POOL_WIDTH, QKV_WIDTH, DN_WIDTH, D_MODEL, D_MODEL, GATE_PAD)
    n_tot = sum(widths)
    tm = TILE_PROJ
    return pl.pallas_call(
        _proj_body,
        grid=(t // tm,),
        in_specs=[pl.BlockSpec((tm, D_MODEL), lambda i: (i, 0)),
                  _resident((1, D_MODEL)), _resident((1, D_MODEL)),
                  _resident((D_MODEL, n_tot))],
        out_specs=[pl.BlockSpec((tm, n), lambda i: (i, 0)) for n in widths],
        out_shape=[jax.ShapeDtypeStruct((t, n), F32) for n in widths],
        compiler_params=pltpu.CompilerParams(dimension_semantics=("arbitrary",),
                                             vmem_limit_bytes=VMEM_LIMIT),
        name="in_projection",
    )(x2, ln_g, ln_b, w_cat)


def _unit_lower_inverse(m, masks):
    eye, base_mask, level_masks = masks
    n = jnp.where(base_mask, m, 0.0)
    p = eye - n
    q = _mm(n, n)
    p = p + _mm(p, q)
    q = _mm(q, q)
    t = p + _mm(p, q)
    for lm in level_masks:
        c = jnp.where(lm, m, 0.0)
        t = t - _mm(t, _mm(c, t))
    return t


def _mixer_body(pool_ref, qkv_ref, z_ref, ga_ref, gb_ref, ba_ref,
                poolw_ref, pscale_ref, convw_ref, alog_ref, dtb_ref, onw_ref,
                out_ref,
                state_ref, ubuf_ref, qbuf_ref, ya_ref):
    ts = out_ref.shape[0]
    n_chunks = ts // CHUNK
    s = pl.program_id(1)

    @pl.when(s == 0)
    def _():
        state_ref[...] = jnp.zeros_like(state_ref)
        ubuf_ref[0:POOL_HIST, :] = jnp.zeros((POOL_HIST, POOL_WIDTH), F32)
        qbuf_ref[0:CONV_HIST, :] = jnp.zeros((CONV_HIST, QKV_WIDTH), F32)

    ubuf_ref[POOL_HIST:POOL_HIST + ts, :] = pool_ref[...]
    qbuf_ref[CONV_HIST:CONV_HIST + ts, :] = qkv_ref[...]

    tpos = s * ts + lax.broadcasted_iota(jnp.int32, (ts, 1), 0)
    for gi, w in enumerate(POOL_WINDOWS):
        cols = slice(gi * POOL_GROUP, (gi + 1) * POOL_GROUP)
        cur = ubuf_ref[POOL_HIST:POOL_HIST + ts, cols]
        acc = cur
        for k in range(1, w):
            acc = acc + ubuf_ref[POOL_HIST - k:POOL_HIST - k + ts, cols]
        cnt = jnp.minimum(tpos + 1, w).astype(F32)
        d = acc / cnt - cur
        ocols = slice(gi * POOL_OUT_GROUP, (gi + 1) * POOL_OUT_GROUP)
        ya = _mm(d, poolw_ref[gi]) * pscale_ref[:, ocols]
        ya_ref[:, ocols] = _sigmoid(ga_ref[:, ocols]) * ya

    ba = ba_ref[...]
    beta_all = _sigmoid(ba)
    lane = lax.broadcasted_iota(jnp.int32, (1, GATE_PAD), 1)
    is_decay = (lane >= DN_HEADS) & (lane < 2 * DN_HEADS)
    a_scale = jnp.where(is_decay, jnp.exp(alog_ref[...]), 0.0)
    g_all = -a_scale * _softplus(ba + dtb_ref[...])
    g_hi = g_all.astype(BF16)
    r1 = g_all - g_hi.astype(F32)
    g_mid = r1.astype(BF16)
    g_lo = (r1 - g_mid.astype(F32)).astype(BF16)
    ri = lax.broadcasted_iota(jnp.int32, (ts, ts), 0)
    ci = lax.broadcasted_iota(jnp.int32, (ts, ts), 1)
    same = (ri // CHUNK) == (ci // CHUNK)
    tri = jnp.concatenate([jnp.where(same & (ri >= ci), 1.0, 0.0),
                           jnp.where(same & (ri < ci), 1.0, 0.0),
                           jnp.where(same, 1.0, 0.0)], axis=0).astype(BF16)
    cum = (jnp.dot(tri, g_hi, preferred_element_type=F32)
           + jnp.dot(tri, g_mid, preferred_element_type=F32)
           + jnp.dot(tri, g_lo, preferred_element_type=F32))
    gc_all = cum[0:ts]
    gr_all = cum[ts:2 * ts]
    gl_all = cum[2 * ts:3 * ts]
    gct_all = gc_all.T

    r = lax.broadcasted_iota(jnp.int32, (CHUNK, CHUNK), 0)
    c = lax.broadcasted_iota(jnp.int32, (CHUNK, CHUNK), 1)
    incl = r >= c
    strict = r > c
    eye = jnp.where(r == c, 1.0, 0.0).astype(F32)
    base_mask = strict & ((r // 8) == (c // 8))
    level_masks = []
    b = 8
    while b < CHUNK:
        level_masks.append(strict & ((r // (2 * b)) == (c // (2 * b))) & ((r // b) != (c // b)))
        b *= 2
    masks = (eye, base_mask, level_masks)
    ones_sq = jnp.ones((LANES, LANES), BF16)

    def conv_silu(rows0, col0):
        acc = None
        for k in range(CONV_K):
            r0 = CONV_HIST + rows0 - (CONV_K - 1) + k
            term = qbuf_ref[r0:r0 + CHUNK, col0:col0 + LANES] * convw_ref[k:k + 1, col0:col0 + LANES]
            acc = term if acc is None else acc + term
        return acc * _sigmoid(acc)

    states = [state_ref[h] for h in range(DN_HEADS)]
    for ch in range(n_chunks):
        rows0 = ch * CHUNK
        rows = slice(rows0, rows0 + CHUNK)
        for h in range(DN_HEADS):
            hc = slice(h * LANES, (h + 1) * LANES)
            q = conv_silu(rows0, h * LANES)
            k = conv_silu(rows0, DN_WIDTH + h * LANES)
            v = conv_silu(rows0, 2 * DN_WIDTH + h * LANES)
            q = q * lax.rsqrt(_mm(q * q, ones_sq) + L2_EPS) * (DN_HEAD_DIM ** -0.5)
            k = k * lax.rsqrt(_mm(k * k, ones_sq) + L2_EPS)

            beta_b = jnp.broadcast_to(beta_all[rows, h:h + 1], (CHUNK, LANES))
            gc_b = jnp.broadcast_to(gc_all[rows, 8 + h:9 + h], (CHUNK, LANES))
            gr_b = jnp.broadcast_to(gr_all[rows, 8 + h:9 + h], (CHUNK, LANES))
            gl_b = jnp.broadcast_to(gl_all[rows, 8 + h:9 + h], (CHUNK, LANES))
            gcrow_b = jnp.broadcast_to(gct_all[8 + h:9 + h, rows], (CHUNK, CHUNK))
            decay = jnp.where(incl, jnp.exp(gc_b - gcrow_b), 0.0)
            egc_b = jnp.exp(gc_b)

            kb = k * beta_b
            aq = _mm_nt(jnp.concatenate([kb, q], axis=0), k)
            m = jnp.where(strict, aq[0:CHUNK] * decay, 0.0)
            attn = aq[CHUNK:2 * CHUNK] * decay
            t_inv = _unit_lower_inverse(m, masks)
            sol = _mm(t_inv, jnp.concatenate([v * beta_b, kb * egc_b], axis=1))
            u = sol[:, 0:LANES]
            w = sol[:, LANES:2 * LANES]
            st = states[h]
            wq = _mm(jnp.concatenate([w, q * egc_b], axis=0), st)
            v_new = u - wq[0:CHUNK]
            o = wq[CHUNK:2 * CHUNK] + _mm(attn, v_new)
            states[h] = st * jnp.exp(gl_b) + _mm_tn(k * jnp.exp(gr_b), v_new)

            ms = _mm(o * o, ones_sq) * (1.0 / DN_HEAD_DIM)
            zz = z_ref[rows, hc]
            yb = o * lax.rsqrt(ms + RMS_EPS) * onw_ref[...] * (zz * _sigmoid(zz))
            out_ref[rows, hc] = (ya_ref[rows, hc] + _sigmoid(gb_ref[rows, hc]) * yb).astype(out_ref.dtype)

    for h in range(DN_HEADS):
        state_ref[h] = states[h]
    ubuf_ref[0:POOL_HIST, :] = ubuf_ref[ts:ts + POOL_HIST, :]
    qbuf_ref[0:CONV_HIST, :] = qbuf_ref[ts:ts + CONV_HIST, :]


def _mixer(pool_in, qkv, z, gate_a, gate_b, ba, pool_w, pool_scale, conv_w, a_vec, dtb_vec, o_norm_w,
           batch, seq):
    ts = TILE_MIX
    n_s = seq // ts
    tok = lambda n: pl.BlockSpec((ts, n), lambda b, s: (b * n_s + s, 0))
    return pl.pallas_call(
        _mixer_body,
        grid=(batch, n_s),
        in_specs=[tok(POOL_WIDTH), tok(QKV_WIDTH), tok(DN_WIDTH), tok(D_MODEL), tok(D_MODEL), tok(GATE_PAD),
                  _resident((N_POOL_GROUPS, POOL_GROUP, POOL_OUT_GROUP)),
                  _resident((1, D_MODEL)), _resident((CONV_K, QKV_WIDTH)),
                  _resident((1, GATE_PAD)), _resident((1, GATE_PAD)), _resident((1, DN_HEAD_DIM))],
        out_specs=tok(D_MODEL),
        out_shape=jax.ShapeDtypeStruct((batch * seq, D_MODEL), BF16),
        scratch_shapes=[pltpu.VMEM((DN_HEADS, DN_HEAD_DIM, DN_HEAD_DIM), F32),
                        pltpu.VMEM((POOL_HIST + ts, POOL_WIDTH), F32),
                        pltpu.VMEM((CONV_HIST + ts, QKV_WIDTH), F32),
                        pltpu.VMEM((ts, D_MODEL), F32)],
        compiler_params=pltpu.CompilerParams(dimension_semantics=("arbitrary", "arbitrary"),
                                             vmem_limit_bytes=VMEM_LIMIT),
        name="mixer",
    )(pool_in, qkv, z, gate_a, gate_b, ba, pool_w, pool_scale, conv_w, a_vec, dtb_vec, o_norm_w)


def _ffn_body(alpha, x_ref, mixed_ref, p_ref, lng_ref, lnb_ref, wout_ref, ln1g_ref, ln1b_ref,
              wup_ref, wdown_ref, wg_ref, wp_ref, ln2g_ref, ln2b_ref, o_ref):
    h = _layer_norm(x_ref[...], lng_ref[...], lnb_ref[...])
    t = alpha * h + jnp.dot(mixed_ref[...], wout_ref[...], preferred_element_type=F32)
    h1 = _layer_norm(t, ln1g_ref[...], ln1b_ref[...])
    h1b = h1.astype(BF16)
    r = alpha * h1
    for c0 in range(0, D_FF, 1024):
        up = jnp.dot(h1b, wup_ref[:, c0:c0 + 1024], preferred_element_type=F32)
        act = jnp.square(jnp.maximum(up, 0.0)).astype(BF16)
        r = r + jnp.dot(act, wdown_ref[c0:c0 + 1024, :], preferred_element_type=F32)
    gate = _sigmoid(jnp.dot(r.astype(BF16), wg_ref[...], preferred_element_type=F32))
    ple = gate * jnp.dot(p_ref[...].astype(BF16), wp_ref[...], preferred_element_type=F32)
    o_ref[...] = _layer_norm(r + ple, ln2g_ref[...], ln2b_ref[...])


def _channel_mixer(alpha, x2, mixed, p2, ln_g, ln_b, w_out, ln1_g, ln1_b, w_up, w_down, w_g, w_p, ln2_g, ln2_b):
    t = x2.shape[0]
    tm = TILE_FFN
    tok = lambda n: pl.BlockSpec((tm, n), lambda i: (i, 0))
    vec = _resident((1, D_MODEL))
    return pl.pallas_call(
        functools.partial(_ffn_body, alpha),
        grid=(t // tm,),
        in_specs=[tok(D_MODEL), tok(D_MODEL), tok(PLE_DIM), vec, vec,
                  _resident((D_MODEL, D_MODEL)), vec, vec,
                  _resident((D_MODEL, D_FF)), _resident((D_FF, D_MODEL)),
                  _resident((D_MODEL, D_MODEL)), _resident((PLE_DIM, D_MODEL)), vec, vec],
        out_specs=tok(D_MODEL),
        out_shape=jax.ShapeDtypeStruct((t, D_MODEL), F32),
        compiler_params=pltpu.CompilerParams(dimension_semantics=("arbitrary",),
                                             vmem_limit_bytes=VMEM_LIMIT),
        name="channel_mixer",
    )(x2, mixed, p2, ln_g, ln_b, w_out, ln1_g, ln1_b, w_up, w_down, w_g, w_p, ln2_g, ln2_b)


def kernel(x, p, ln_in_g, ln_in_b, w_in, pool_w, pool_scale, conv_w, a_log, dt_bias, o_norm_w, w_out,
           ln1_g, ln1_b, w_up, w_down, ple_gate_w, ple_proj_w, ln2_g, ln2_b):
    batch, seq, _ = x.shape
    depth = w_in.shape[0]
    assert depth == 1, "the fused input layer norm assumes a single layer"
    alpha = (2.0 * depth) ** 0.25
    tokens = batch * seq
    x2 = x.reshape(tokens, D_MODEL)
    row = lambda v: v.reshape(1, -1).astype(F32)

    i = 0
    o_z = POOL_WIDTH + QKV_WIDTH
    o_beta = o_z + DN_WIDTH
    o_ga = o_beta + 2 * DN_HEADS
    w = w_in[i]
    w_cat = jnp.concatenate(
        [w[:, :o_beta], w[:, o_ga:], w[:, o_beta:o_ga],
         jnp.zeros((D_MODEL, GATE_PAD - 2 * DN_HEADS), w.dtype)], axis=1).astype(BF16)
    pool_in, qkv, z, gate_a, gate_b, ba = _in_projection(x2, row(ln_in_g), row(ln_in_b), w_cat)

    pad = jnp.zeros((GATE_PAD - 2 * DN_HEADS,), F32)
    zeros_h = jnp.zeros((DN_HEADS,), F32)
    a_vec = jnp.concatenate([zeros_h, a_log[i].astype(F32), pad]).reshape(1, GATE_PAD)
    dtb_vec = jnp.concatenate([zeros_h, dt_bias[i].astype(F32), pad]).reshape(1, GATE_PAD)
    mixed = _mixer(pool_in, qkv, z, gate_a, gate_b, ba, pool_w[i].astype(BF16), row(pool_scale[i]),
                   conv_w[i].astype(F32), a_vec, dtb_vec, row(o_norm_w[i]), batch, seq)

    out = _channel_mixer(alpha, x2, mixed, p[i].reshape(tokens, PLE_DIM), row(ln_in_g), row(ln_in_b),
                         w_out[i].astype(BF16), row(ln1_g[i]), row(ln1_b[i]),
                         w_up[i].astype(BF16), w_down[i].astype(BF16),
                         ple_gate_w[i].astype(BF16), ple_proj_w[i].astype(BF16),
                         row(ln2_g[i]), row(ln2_b[i]))
    return out.reshape(batch, seq, D_MODEL)
```

```python
import functools

import jax
import jax.numpy as jnp
from jax import lax
from jax.experimental import pallas as pl
from jax.experimental.pallas import tpu as pltpu

F32 = jnp.float32
BF16 = jnp.bfloat16

D_MODEL = 1024
POOL_WINDOWS = (2, 4, 8, 16)
N_POOL_GROUPS = 4
POOL_WIDTH = D_MODEL // 2
POOL_GROUP = POOL_WIDTH // N_POOL_GROUPS
POOL_OUT_GROUP = D_MODEL // N_POOL_GROUPS
DN_HEADS = 8
DN_HEAD_DIM = 128
DN_WIDTH = DN_HEADS * DN_HEAD_DIM
CONV_K = 4
D_FF = 4 * D_MODEL
PLE_DIM = 256
LN_EPS = 1e-5
RMS_EPS = 1e-6
L2_EPS = 1e-6
QKV_WIDTH = 3 * DN_WIDTH
LOG2E = 1.4426950408889634

LANES = 128
PAIR = 2 * LANES
N_PAIRS = DN_HEADS // 2
GATE_PAD = LANES
POOL_HIST = 16
CONV_HIST = 8

CHUNK = 128
TILE_PROJ = 256
TILE_MIX = 256
TILE_FFN = 512
VMEM_LIMIT = 56 * 1024 * 1024

_O_QKV = POOL_WIDTH
_O_Z = _O_QKV + QKV_WIDTH
_O_GA = _O_Z + DN_WIDTH
_O_GB = _O_GA + D_MODEL
_O_BG = _O_GB + D_MODEL
_N_PROJ = _O_BG + GATE_PAD


def _layer_norm(x, g, b):
    mu = jnp.mean(x, axis=-1, keepdims=True)
    xc = x - mu
    var = jnp.mean(xc * xc, axis=-1, keepdims=True)
    return xc * lax.rsqrt(var + LN_EPS) * g + b


def _sigmoid(x):
    return 1.0 / (1.0 + jnp.exp2(x * (-LOG2E)))


def _softplus(x):
    return jnp.maximum(x, 0.0) + jnp.log1p(jnp.exp(-jnp.abs(x)))


def _dot(a, b):
    return jnp.dot(a, b, preferred_element_type=F32)


def _resident(shape):
    return pl.BlockSpec(shape, lambda *_: (0,) * len(shape), pipeline_mode=pl.Buffered(1))


def _proj_body(tiles_per_seq, x_ref, g_ref, b_ref, w_ref, poolw_ref, pscale_ref, convw_ref, alog_ref, dtb_ref,
               yag_ref, q_ref, k_ref, v_ref, sz_ref, sgb_ref, bg_ref,
               ubuf_ref, qbuf_ref):
    tm = x_ref.shape[0]
    s = pl.program_id(0) % tiles_per_seq

    @pl.when(s == 0)
    def _():
        ubuf_ref[0:POOL_HIST, :] = jnp.zeros((POOL_HIST, POOL_WIDTH), F32)
        qbuf_ref[0:CONV_HIST, :] = jnp.zeros((CONV_HIST, QKV_WIDTH), F32)

    h = _layer_norm(x_ref[...], g_ref[...], b_ref[...]).astype(BF16)

    def proj(c0, n):
        return _dot(h, w_ref[:, c0:c0 + n])

    ubuf_ref[POOL_HIST:POOL_HIST + tm, :] = proj(0, POOL_WIDTH)
    tpos = s * tm + lax.broadcasted_iota(jnp.int32, (tm, 1), 0)
    for gi, w in enumerate(POOL_WINDOWS):
        cols = slice(gi * POOL_GROUP, (gi + 1) * POOL_GROUP)
        cur = ubuf_ref[POOL_HIST:POOL_HIST + tm, cols]
        acc = cur
        for k in range(1, w):
            acc = acc + ubuf_ref[POOL_HIST - k:POOL_HIST - k + tm, cols]
        cnt = jnp.minimum(tpos + 1, w).astype(F32)
        d = acc / cnt - cur
        ocols = slice(gi * POOL_OUT_GROUP, (gi + 1) * POOL_OUT_GROUP)
        ya = _dot(d.astype(BF16), poolw_ref[gi]) * pscale_ref[:, ocols]
        yag_ref[:, ocols] = _sigmoid(proj(_O_GA + gi * POOL_OUT_GROUP, POOL_OUT_GROUP)) * ya
    ubuf_ref[0:POOL_HIST, :] = ubuf_ref[tm:tm + POOL_HIST, :]

    for c0 in range(0, QKV_WIDTH, 1024):
        qbuf_ref[CONV_HIST:CONV_HIST + tm, c0:c0 + 1024] = proj(_O_QKV + c0, 1024)
    for grp in range(QKV_WIDTH // LANES):
        cols = slice(grp * LANES, (grp + 1) * LANES)
        acc = None
        for k in range(CONV_K):
            r0 = CONV_HIST - (CONV_K - 1) + k
            term = qbuf_ref[r0:r0 + tm, cols] * convw_ref[k:k + 1, cols]
            acc = term if acc is None else acc + term
        y = acc * _sigmoid(acc)
        hcols = slice((grp % DN_HEADS) * LANES, (grp % DN_HEADS + 1) * LANES)
        if grp < DN_HEADS:
            q_ref[:, hcols] = y * (lax.rsqrt(jnp.sum(y * y, axis=-1, keepdims=True) + L2_EPS)
                                   * (DN_HEAD_DIM ** -0.5))
        elif grp < 2 * DN_HEADS:
            k_ref[:, hcols] = y * lax.rsqrt(jnp.sum(y * y, axis=-1, keepdims=True) + L2_EPS)
        else:
            v_ref[:, hcols] = y
    qbuf_ref[0:CONV_HIST, :] = qbuf_ref[tm:tm + CONV_HIST, :]

    z = proj(_O_Z, DN_WIDTH)
    sz_ref[...] = z * _sigmoid(z)
    sgb_ref[...] = _sigmoid(proj(_O_GB, D_MODEL))
    ba = proj(_O_BG, GATE_PAD)
    lane = lax.broadcasted_iota(jnp.int32, (1, GATE_PAD), 1)
    is_decay = (lane >= DN_HEADS) & (lane < 2 * DN_HEADS)
    a_scale = jnp.where(is_decay, jnp.exp(alog_ref[...]), 0.0)
    g = -a_scale * _softplus(ba + dtb_ref[...])
    bg_ref[...] = jnp.where(lane < DN_HEADS, _sigmoid(ba), g)


def _in_projection(x2, ln_g, ln_b, w_cat, pool_w, pool_scale, conv_w, alog_vec, dtb_vec, seq):
    t = x2.shape[0]
    tm = TILE_PROJ
    tok = lambda n: pl.BlockSpec((tm, n), lambda i: (i, 0))
    widths = (D_MODEL, DN_WIDTH, DN_WIDTH, DN_WIDTH, DN_WIDTH, D_MODEL, GATE_PAD)
    return pl.pallas_call(
        functools.partial(_proj_body, seq // tm),
        grid=(t // tm,),
        in_specs=[tok(D_MODEL), _resident((1, D_MODEL)), _resident((1, D_MODEL)),
                  _resident((D_MODEL, _N_PROJ)),
                  _resident((N_POOL_GROUPS, POOL_GROUP, POOL_OUT_GROUP)), _resident((1, D_MODEL)),
                  _resident((CONV_K, QKV_WIDTH)), _resident((1, GATE_PAD)), _resident((1, GATE_PAD))],
        out_specs=[tok(n) for n in widths],
        out_shape=[jax.ShapeDtypeStruct((t, n), F32) for n in widths],
        scratch_shapes=[pltpu.VMEM((POOL_HIST + tm, POOL_WIDTH), F32),
                        pltpu.VMEM((CONV_HIST + tm, QKV_WIDTH), F32)],
        compiler_params=pltpu.CompilerParams(dimension_semantics=("arbitrary",),
                                             vmem_limit_bytes=VMEM_LIMIT),
        name="in_projection",
    )(x2, ln_g, ln_b, w_cat, pool_w, pool_scale, conv_w, alog_vec, dtb_vec)


def _block_diag(a2):
    z = jnp.zeros((a2.shape[0], LANES), a2.dtype)
    return jnp.concatenate([jnp.concatenate([a2[:, :LANES], z], axis=1),
                            jnp.concatenate([z, a2[:, LANES:]], axis=1)], axis=0)


def _pair_dot(a2, b2):
    return _dot(a2, _block_diag(b2))


def _unit_lower_inverse(ms, masks):
    eye, base_mask, level_masks = masks
    n = [jnp.where(base_mask, m, 0.0) for m in ms]
    p = [eye - x for x in n]
    nb = [x.astype(BF16) for x in n]
    qb = [_pair_dot(x, x).astype(BF16) for x in nb]
    p = [x + _pair_dot(x.astype(BF16), y) for x, y in zip(p, qb)]
    qb = [_pair_dot(x, x).astype(BF16) for x in qb]
    t = [x + _pair_dot(x.astype(BF16), y) for x, y in zip(p, qb)]
    for lm in level_masks:
        cb = [jnp.where(lm, m, 0.0).astype(BF16) for m in ms]
        tb = [x.astype(BF16) for x in t]
        yb = [_pair_dot(c, x).astype(BF16) for c, x in zip(cb, tb)]
        t = [x - _pair_dot(xb, y) for x, xb, y in zip(t, tb, yb)]
    return t


def _mixer_body(q_ref, k_ref, v_ref, sz_ref, sgb_ref, yag_ref, bg_ref, onw_ref, out_ref, state_ref):
    ts = out_ref.shape[0]
    n_chunks = ts // CHUNK

    @pl.when(pl.program_id(1) == 0)
    def _():
        state_ref[...] = jnp.zeros_like(state_ref)

    bg = bg_ref[...]
    b_hi = bg.astype(BF16)
    r1 = bg - b_hi.astype(F32)
    b_mid = r1.astype(BF16)
    b_lo = (r1 - b_mid.astype(F32)).astype(BF16)
    ri = lax.broadcasted_iota(jnp.int32, (ts, ts), 0)
    ci = lax.broadcasted_iota(jnp.int32, (ts, ts), 1)
    same = (ri // CHUNK) == (ci // CHUNK)
    tri = jnp.concatenate([jnp.where(same & (ri >= ci), 1.0, 0.0),
                           jnp.where(same & (ri < ci), 1.0, 0.0),
                           jnp.where(same, 1.0, 0.0)], axis=0).astype(BF16)
    cum = _dot(tri, b_hi) + _dot(tri, b_mid) + _dot(tri, b_lo)
    gc_all = cum[0:ts]
    gr_all = cum[ts:2 * ts]
    gl_all = cum[2 * ts:3 * ts]
    gct_all = gc_all.T

    r = lax.broadcasted_iota(jnp.int32, (CHUNK, PAIR), 0)
    c = lax.broadcasted_iota(jnp.int32, (CHUNK, PAIR), 1) % LANES
    incl = r >= c
    strict = r > c
    eye = jnp.where(r == c, 1.0, 0.0).astype(F32)
    base_mask = strict & ((r // 8) == (c // 8))
    level_masks = []
    b = 8
    while b < CHUNK:
        level_masks.append(strict & ((r // (2 * b)) == (c // (2 * b))) & ((r // b) != (c // b)))
        b *= 2
    masks = (eye, base_mask, level_masks)
    ones_bd = _block_diag(jnp.ones((LANES, PAIR), BF16))
    onw2 = jnp.concatenate([onw_ref[...], onw_ref[...]], axis=1)

    def lanes_of(arr, rows, p, lane0):
        return jnp.concatenate(
            [jnp.broadcast_to(arr[rows, lane0 + 2 * p + j:lane0 + 2 * p + j + 1], (CHUNK, LANES))
             for j in range(2)], axis=1)

    probs = [(ch, p) for ch in range(n_chunks) for p in range(N_PAIRS)]
    rows_of = lambda ch: slice(ch * CHUNK, (ch + 1) * CHUNK)
    cols_of = lambda p: slice(p * PAIR, (p + 1) * PAIR)

    k2 = [k_ref[rows_of(ch), cols_of(p)] for ch, p in probs]
    beta2 = [lanes_of(bg, rows_of(ch), p, 0) for ch, p in probs]
    gc2 = [lanes_of(gc_all, rows_of(ch), p, DN_HEADS) for ch, p in probs]
    gcrow2 = [jnp.concatenate(
        [jnp.broadcast_to(gct_all[DN_HEADS + 2 * p + j:DN_HEADS + 2 * p + j + 1, rows_of(ch)], (CHUNK, CHUNK))
         for j in range(2)], axis=1) for ch, p in probs]
    decay2 = [jnp.where(incl, jnp.exp(a - b), 0.0) for a, b in zip(gc2, gcrow2)]
    egc2 = [jnp.exp(x) for x in gc2]
    kb2 = [x * y for x, y in zip(k2, beta2)]
    kbf = [x.astype(BF16) for x in k2]
    qbf = [q_ref[rows_of(ch), cols_of(p)].astype(BF16) for ch, p in probs]
    aq = [lax.dot_general(jnp.concatenate([x.astype(BF16), y], axis=0), _block_diag(z),
                          (((1,), (1,)), ((), ())), preferred_element_type=F32)
          for x, y, z in zip(kb2, qbf, kbf)]
    m2 = [jnp.where(strict, x[0:CHUNK] * d, 0.0) for x, d in zip(aq, decay2)]
    attn_b = [(x[CHUNK:2 * CHUNK] * d).astype(BF16) for x, d in zip(aq, decay2)]
    t_b = [x.astype(BF16) for x in _unit_lower_inverse(m2, masks)]
    u2 = [_pair_dot(t, (v_ref[rows_of(ch), cols_of(p)] * bt).astype(BF16))
          for t, bt, (ch, p) in zip(t_b, beta2, probs)]
    w2 = [_pair_dot(t, (x * e).astype(BF16)) for t, x, e in zip(t_b, kb2, egc2)]
    wqg_b = [jnp.concatenate([w.astype(BF16), (q_ref[rows_of(ch), cols_of(p)] * e).astype(BF16)], axis=0)
             for w, e, (ch, p) in zip(w2, egc2, probs)]
    kgt_b = [(x * jnp.exp(lanes_of(gr_all, rows_of(ch), p, DN_HEADS))).T.astype(BF16)
             for x, (ch, p) in zip(k2, probs)]
    egl2 = [jnp.exp(lanes_of(gl_all, rows_of(ch), p, DN_HEADS)) for ch, p in probs]

    states = [state_ref[p] for p in range(N_PAIRS)]
    for ch in range(n_chunks):
        rows = rows_of(ch)
        idx = [ch * N_PAIRS + p for p in range(N_PAIRS)]
        wq = [_pair_dot(wqg_b[i], states[p].astype(BF16)) for p, i in enumerate(idx)]
        v_new_b = [(u2[i] - x[0:CHUNK]).astype(BF16) for x, i in zip(wq, idx)]
        o2 = [x[CHUNK:2 * CHUNK] + _pair_dot(attn_b[i], vn) for x, vn, i in zip(wq, v_new_b, idx)]
        full = [_dot(kgt_b[i], vn) for vn, i in zip(v_new_b, idx)]
        states = [st * egl2[i] + jnp.concatenate([f[0:LANES, 0:LANES], f[LANES:PAIR, LANES:PAIR]], axis=1)
                  for st, f, i in zip(states, full, idx)]
        ms = [_dot((x * x).astype(BF16), ones_bd) * (1.0 / DN_HEAD_DIM) for x in o2]
        for p in range(N_PAIRS):
            cols = cols_of(p)
            yb = o2[p] * lax.rsqrt(ms[p] + RMS_EPS) * onw2 * sz_ref[rows, cols]
            out_ref[rows, cols] = (yag_ref[rows, cols] + sgb_ref[rows, cols] * yb).astype(out_ref.dtype)

    for p in range(N_PAIRS):
        state_ref[p] = states[p]


def _mixer(q, k, v, sz, sgb, yag, bg, o_norm_w, batch, seq):
    ts = TILE_MIX
    n_s = seq // ts
    tok = lambda n: pl.BlockSpec((ts, n), lambda b, s: (b * n_s + s, 0))
    return pl.pallas_call(
        _mixer_body,
        grid=(batch, n_s),
        in_specs=[tok(DN_WIDTH), tok(DN_WIDTH), tok(DN_WIDTH), tok(DN_WIDTH), tok(D_MODEL), tok(D_MODEL),
                  tok(GATE_PAD), _resident((1, DN_HEAD_DIM))],
        out_specs=tok(D_MODEL),
        out_shape=jax.ShapeDtypeStruct((batch * seq, D_MODEL), BF16),
        scratch_shapes=[pltpu.VMEM((N_PAIRS, DN_HEAD_DIM, PAIR), F32)],
        compiler_params=pltpu.CompilerParams(dimension_semantics=("arbitrary", "arbitrary"),
                                             vmem_limit_bytes=VMEM_LIMIT),
        name="mixer",
    )(q, k, v, sz, sgb, yag, bg, o_norm_w)


def _ffn_body(alpha, x_ref, mixed_ref, p_ref, lng_ref, lnb_ref, wout_ref, ln1g_ref, ln1b_ref,
              wup_ref, wdown_ref, wg_ref, wp_ref, ln2g_ref, ln2b_ref, o_ref):
    h = _layer_norm(x_ref[...], lng_ref[...], lnb_ref[...])
    t = alpha * h + _dot(mixed_ref[...], wout_ref[...])
    h1 = _layer_norm(t, ln1g_ref[...], ln1b_ref[...])
    h1b = h1.astype(BF16)
    r = alpha * h1
    for c0 in range(0, D_FF, 1024):
        up = _dot(h1b, wup_ref[:, c0:c0 + 1024])
        act = jnp.square(jnp.maximum(up, 0.0)).astype(BF16)
        r = r + _dot(act, wdown_ref[c0:c0 + 1024, :])
    gate = _sigmoid(_dot(r.astype(BF16), wg_ref[...]))
    ple = gate * _dot(p_ref[...].astype(BF16), wp_ref[...])
    o_ref[...] = _layer_norm(r + ple, ln2g_ref[...], ln2b_ref[...])


def _channel_mixer(alpha, x2, mixed, p2, ln_g, ln_b, w_out, ln1_g, ln1_b, w_up, w_down, w_g, w_p, ln2_g, ln2_b):
    t = x2.shape[0]
    tm = TILE_FFN
    tok = lambda n: pl.BlockSpec((tm, n), lambda i: (i, 0))
    vec = _resident((1, D_MODEL))
    return pl.pallas_call(
        functools.partial(_ffn_body, alpha),
        grid=(t // tm,),
        in_specs=[tok(D_MODEL), tok(D_MODEL), tok(PLE_DIM), vec, vec,
                  _resident((D_MODEL, D_MODEL)), vec, vec,
                  _resident((D_MODEL, D_FF)), _resident((D_FF, D_MODEL)),
                  _resident((D_MODEL, D_MODEL)), _resident((PLE_DIM, D_MODEL)), vec, vec],
        out_specs=tok(D_MODEL),
        out_shape=jax.ShapeDtypeStruct((t, D_MODEL), F32),
        compiler_params=pltpu.CompilerParams(dimension_semantics=("arbitrary",),
                                             vmem_limit_bytes=VMEM_LIMIT),
        name="channel_mixer",
    )(x2, mixed, p2, ln_g, ln_b, w_out, ln1_g, ln1_b, w_up, w_down, w_g, w_p, ln2_g, ln2_b)


def kernel(x, p, ln_in_g, ln_in_b, w_in, pool_w, pool_scale, conv_w, a_log, dt_bias, o_norm_w, w_out,
           ln1_g, ln1_b, w_up, w_down, ple_gate_w, ple_proj_w, ln2_g, ln2_b):
    batch, seq, _ = x.shape
    depth = w_in.shape[0]
    assert depth == 1, "the fused input layer norm assumes a single layer"
    alpha = (2.0 * depth) ** 0.25
    tokens = batch * seq
    x2 = x.reshape(tokens, D_MODEL)
    row = lambda v: v.reshape(1, -1).astype(F32)

    i = 0
    o_beta = POOL_WIDTH + QKV_WIDTH + DN_WIDTH
    o_ga = o_beta + 2 * DN_HEADS
    w = w_in[i]
    w_cat = jnp.concatenate(
        [w[:, :o_beta], w[:, o_ga:], w[:, o_beta:o_ga],
         jnp.zeros((D_MODEL, GATE_PAD - 2 * DN_HEADS), w.dtype)], axis=1).astype(BF16)
    pad = jnp.zeros((GATE_PAD - 2 * DN_HEADS,), F32)
    zeros_h = jnp.zeros((DN_HEADS,), F32)
    alog_vec = jnp.concatenate([zeros_h, a_log[i].astype(F32), pad]).reshape(1, GATE_PAD)
    dtb_vec = jnp.concatenate([zeros_h, dt_bias[i].astype(F32), pad]).reshape(1, GATE_PAD)
    yag, q, k, v, sz, sgb, bg = _in_projection(
        x2, row(ln_in_g), row(ln_in_b), w_cat, pool_w[i].astype(BF16), row(pool_scale[i]),
        conv_w[i].astype(F32), alog_vec, dtb_vec, seq)

    mixed = _mixer(q, k, v, sz, sgb, yag, bg, row(o_norm_w[i]), batch, seq)

    out = _channel_mixer(alpha, x2, mixed, p[i].reshape(tokens, PLE_DIM), row(ln_in_g), row(ln_in_b),
                         w_out[i].astype(BF16), row(ln1_g[i]), row(ln1_b[i]),
                         w_up[i].astype(BF16), w_down[i].astype(BF16),
                         ple_gate_w[i].astype(BF16), ple_proj_w[i].astype(BF16),
                         row(ln2_g[i]), row(ln2_b[i]))
    return out.reshape(batch, seq, D_MODEL)
```

```python
import functools

import jax
import jax.numpy as jnp
from jax import lax
from jax.experimental import pallas as pl
from jax.experimental.pallas import tpu as pltpu

F32 = jnp.float32
BF16 = jnp.bfloat16

D_MODEL = 1024
POOL_WINDOWS = (2, 4, 8, 16)
N_POOL_GROUPS = 4
POOL_WIDTH = D_MODEL // 2
POOL_GROUP = POOL_WIDTH // N_POOL_GROUPS
POOL_OUT_GROUP = D_MODEL // N_POOL_GROUPS
DN_HEADS = 8
DN_HEAD_DIM = 128
DN_WIDTH = DN_HEADS * DN_HEAD_DIM
CONV_K = 4
D_FF = 4 * D_MODEL
PLE_DIM = 256
LN_EPS = 1e-5
RMS_EPS = 1e-6
L2_EPS = 1e-6
QKV_WIDTH = 3 * DN_WIDTH
LOG2E = 1.4426950408889634

LANES = 128
SUBLANES = 8
PAIR = 2 * LANES
N_PAIRS = DN_HEADS // 2
GATE_PAD = LANES
POOL_HIST = 16
CONV_HIST = 8

CHUNK = 128
TILE_PROJ = 256
TILE_MIX = 256
TILE_FFN = 512
VMEM_LIMIT = 56 * 1024 * 1024

_O_QKV = POOL_WIDTH
_O_Z = _O_QKV + QKV_WIDTH
_O_GA = _O_Z + DN_WIDTH
_O_GB = _O_GA + D_MODEL
_O_BG = _O_GB + D_MODEL
_N_PROJ = _O_BG + GATE_PAD


def _layer_norm(x, g, b):
    mu = jnp.mean(x, axis=-1, keepdims=True)
    xc = x - mu
    var = jnp.mean(xc * xc, axis=-1, keepdims=True)
    return xc * lax.rsqrt(var + LN_EPS) * g + b


def _sigmoid(x):
    return 1.0 / (1.0 + jnp.exp2(x * (-LOG2E)))


def _softplus(x):
    return jnp.maximum(x, 0.0) + jnp.log1p(jnp.exp(-jnp.abs(x)))


def _dot(a, b):
    return jnp.dot(a, b, preferred_element_type=F32)


def _resident(shape):
    return pl.BlockSpec(shape, lambda *_: (0,) * len(shape), pipeline_mode=pl.Buffered(1))


def _zero_row_after(y):
    bits = pltpu.bitcast(y[0:SUBLANES, 0:LANES], jnp.uint32)
    zero = lax.shift_right_logical(lax.shift_right_logical(bits, jnp.uint32(16)), jnp.uint32(16))
    return pltpu.bitcast(zero, F32)[0:1]


def _proj_body(tiles_per_seq, x_ref, g_ref, b_ref, w_ref, poolw_ref, pscale_ref, convw_ref, alog_ref, dtb_ref,
               yag_ref, q_ref, k_ref, v_ref, sz_ref, sgb_ref, bg_ref,
               ubuf_ref, qbuf_ref):
    tm = x_ref.shape[0]
    s = pl.program_id(0) % tiles_per_seq

    @pl.when(s == 0)
    def _():
        ubuf_ref[0:POOL_HIST, :] = jnp.zeros((POOL_HIST, POOL_WIDTH), F32)
        qbuf_ref[0:CONV_HIST, :] = jnp.zeros((CONV_HIST, QKV_WIDTH), F32)

    h = _layer_norm(x_ref[...], g_ref[...], b_ref[...]).astype(BF16)

    def proj(c0, n):
        return _dot(h, w_ref[:, c0:c0 + n])

    tpos = s * tm + lax.broadcasted_iota(jnp.int32, (tm, 1), 0)

    def pool_slice(c0):
        ubuf_ref[POOL_HIST:POOL_HIST + tm, c0:c0 + PAIR] = proj(c0, PAIR)
        return None

    def pool_group(gi):
        w = POOL_WINDOWS[gi]
        ue = ubuf_ref[0:POOL_HIST + tm, gi * POOL_GROUP:(gi + 1) * POOL_GROUP]
        win, span = ue, 1
        while span < w:
            win = win + pltpu.roll(win, span, axis=0)
            span *= 2
        cnt = jnp.minimum(tpos + 1, w).astype(F32)
        d = win[POOL_HIST:POOL_HIST + tm] / cnt - ue[POOL_HIST:POOL_HIST + tm]
        ocols = slice(gi * POOL_OUT_GROUP, (gi + 1) * POOL_OUT_GROUP)
        ya = _dot(d.astype(BF16), poolw_ref[gi]) * pscale_ref[:, ocols]
        ga = proj(_O_GA + gi * POOL_OUT_GROUP, POOL_OUT_GROUP)
        yag_ref[:, ocols] = _sigmoid(ga) * ya
        return _zero_row_after(ga)

    def gate_slice(ref, o0, c0, is_silu):
        pre = proj(o0 + c0, PAIR)
        act = _sigmoid(pre)
        ref[:, c0:c0 + PAIR] = pre * act if is_silu else act
        return _zero_row_after(pre)

    for c0 in range(0, POOL_WIDTH, PAIR):
        pool_slice(c0)
    for gi in range(N_POOL_GROUPS):
        pool_group(gi)

    side = ([functools.partial(gate_slice, sz_ref, _O_Z, c0, True) for c0 in range(0, DN_WIDTH, PAIR)]
            + [functools.partial(gate_slice, sgb_ref, _O_GB, c0, False) for c0 in range(0, D_MODEL, PAIR)])
    n_groups = QKV_WIDTH // LANES
    n_slices = QKV_WIDTH // PAIR
    after = None

    for grp in range(n_groups):
        if grp % 2 == 0:
            sl = grp // 2
            qbuf_ref[CONV_HIST:CONV_HIST + tm, sl * PAIR:(sl + 1) * PAIR] = proj(_O_QKV + sl * PAIR, PAIR)
            for unit in side[sl * len(side) // n_slices:(sl + 1) * len(side) // n_slices]:
                dep = unit()
                after = dep if dep is not None else after
        cols = slice(grp * LANES, (grp + 1) * LANES)
        xe = qbuf_ref[0:CONV_HIST + tm, cols]
        acc = None
        for k in range(CONV_K):
            back = CONV_K - 1 - k
            xs = xe if back == 0 else pltpu.roll(xe, back, axis=0)
            tap = convw_ref[k:k + 1, cols]
            if k == 0 and after is not None:
                tap = tap + after
            term = xs[CONV_HIST:CONV_HIST + tm] * tap
            acc = term if acc is None else acc + term
        y = acc * _sigmoid(acc)
        hcols = slice((grp % DN_HEADS) * LANES, (grp % DN_HEADS + 1) * LANES)
        if grp < DN_HEADS:
            q_ref[:, hcols] = y * (lax.rsqrt(jnp.sum(y * y, axis=-1, keepdims=True) + L2_EPS)
                                   * (DN_HEAD_DIM ** -0.5))
        elif grp < 2 * DN_HEADS:
            k_ref[:, hcols] = y * lax.rsqrt(jnp.sum(y * y, axis=-1, keepdims=True) + L2_EPS)
        else:
            v_ref[:, hcols] = y
    qbuf_ref[0:CONV_HIST, :] = qbuf_ref[tm:tm + CONV_HIST, :]
    ubuf_ref[0:POOL_HIST, :] = ubuf_ref[tm:tm + POOL_HIST, :]

    ba = proj(_O_BG, GATE_PAD)
    lane = lax.broadcasted_iota(jnp.int32, (1, GATE_PAD), 1)
    is_decay = (lane >= DN_HEADS) & (lane < 2 * DN_HEADS)
    a_scale = jnp.where(is_decay, jnp.exp(alog_ref[...]), 0.0)
    g = -a_scale * _softplus(ba + dtb_ref[...])
    bg_ref[...] = jnp.where(lane < DN_HEADS, _sigmoid(ba), g)


def _in_projection(x2, ln_g, ln_b, w_cat, pool_w, pool_scale, conv_w, alog_vec, dtb_vec, seq):
    t = x2.shape[0]
    tm = TILE_PROJ
    tok = lambda n: pl.BlockSpec((tm, n), lambda i: (i, 0))
    widths = (D_MODEL, DN_WIDTH, DN_WIDTH, DN_WIDTH, DN_WIDTH, D_MODEL, GATE_PAD)
    return pl.pallas_call(
        functools.partial(_proj_body, seq // tm),
        grid=(t // tm,),
        in_specs=[tok(D_MODEL), _resident((1, D_MODEL)), _resident((1, D_MODEL)),
                  _resident((D_MODEL, _N_PROJ)),
                  _resident((N_POOL_GROUPS, POOL_GROUP, POOL_OUT_GROUP)), _resident((1, D_MODEL)),
                  _resident((CONV_K, QKV_WIDTH)), _resident((1, GATE_PAD)), _resident((1, GATE_PAD))],
        out_specs=[tok(n) for n in widths],
        out_shape=[jax.ShapeDtypeStruct((t, n), F32) for n in widths],
        scratch_shapes=[pltpu.VMEM((POOL_HIST + tm, POOL_WIDTH), F32),
                        pltpu.VMEM((CONV_HIST + tm, QKV_WIDTH), F32)],
        compiler_params=pltpu.CompilerParams(dimension_semantics=("arbitrary",),
                                             vmem_limit_bytes=VMEM_LIMIT),
        name="in_projection",
    )(x2, ln_g, ln_b, w_cat, pool_w, pool_scale, conv_w, alog_vec, dtb_vec)


def _block_diag(a2):
    z = jnp.zeros((a2.shape[0], LANES), a2.dtype)
    return jnp.concatenate([jnp.concatenate([a2[:, :LANES], z], axis=1),
                            jnp.concatenate([z, a2[:, LANES:]], axis=1)], axis=0)


def _pair_dot(a2, b2):
    return _dot(a2, _block_diag(b2))


def _unit_lower_inverse(ms, masks):
    eye, base_mask, level_masks = masks
    n = [jnp.where(base_mask, m, 0.0) for m in ms]
    p = [eye - x for x in n]
    nb = [x.astype(BF16) for x in n]
    qb = [_pair_dot(x, x).astype(BF16) for x in nb]
    p = [x + _pair_dot(x.astype(BF16), y) for x, y in zip(p, qb)]
    qb = [_pair_dot(x, x).astype(BF16) for x in qb]
    t = [x + _pair_dot(x.astype(BF16), y) for x, y in zip(p, qb)]
    for lm in level_masks:
        cb = [jnp.where(lm, m, 0.0).astype(BF16) for m in ms]
        tb = [x.astype(BF16) for x in t]
        yb = [_pair_dot(c, x).astype(BF16) for c, x in zip(cb, tb)]
        t = [x - _pair_dot(xb, y) for x, xb, y in zip(t, tb, yb)]
    return t


def _mixer_body(q_ref, k_ref, v_ref, sz_ref, sgb_ref, yag_ref, bg_ref, onw_ref, out_ref, state_ref):
    ts = out_ref.shape[0]
    n_chunks = ts // CHUNK

    @pl.when(pl.program_id(1) == 0)
    def _():
        state_ref[...] = jnp.zeros_like(state_ref)

    bg = bg_ref[...]
    b_hi = bg.astype(BF16)
    r1 = bg - b_hi.astype(F32)
    b_mid = r1.astype(BF16)
    b_lo = (r1 - b_mid.astype(F32)).astype(BF16)
    ri = lax.broadcasted_iota(jnp.int32, (ts, ts), 0)
    ci = lax.broadcasted_iota(jnp.int32, (ts, ts), 1)
    tri = jnp.where(((ri // CHUNK) == (ci // CHUNK)) & (ri >= ci), 1.0, 0.0).astype(BF16)
    gc_all = _dot(tri, b_hi) + _dot(tri, b_mid) + _dot(tri, b_lo)
    gct_all = gc_all.T

    r = lax.broadcasted_iota(jnp.int32, (CHUNK, PAIR), 0)
    c = lax.broadcasted_iota(jnp.int32, (CHUNK, PAIR), 1) % LANES
    incl = r >= c
    strict = r > c
    eye = jnp.where(r == c, 1.0, 0.0).astype(F32)
    base_mask = strict & ((r // 8) == (c // 8))
    level_masks = []
    b = 8
    while b < CHUNK:
        level_masks.append(strict & ((r // (2 * b)) == (c // (2 * b))) & ((r // b) != (c // b)))
        b *= 2
    masks = (eye, base_mask, level_masks)
    ones_bd = _block_diag(jnp.ones((LANES, PAIR), BF16))
    onw2 = jnp.concatenate([onw_ref[...], onw_ref[...]], axis=1)

    def lanes_of(arr, rows, p, lane0):
        return jnp.concatenate(
            [jnp.broadcast_to(arr[rows, lane0 + 2 * p + j:lane0 + 2 * p + j + 1], (CHUNK, LANES))
             for j in range(2)], axis=1)

    probs = [(ch, p) for ch in range(n_chunks) for p in range(N_PAIRS)]
    rows_of = lambda ch: slice(ch * CHUNK, (ch + 1) * CHUNK)
    last_of = lambda ch: slice((ch + 1) * CHUNK - 1, (ch + 1) * CHUNK)
    cols_of = lambda p: slice(p * PAIR, (p + 1) * PAIR)

    k2 = [k_ref[rows_of(ch), cols_of(p)] for ch, p in probs]
    beta2 = [lanes_of(bg, rows_of(ch), p, 0) for ch, p in probs]
    gc2 = [lanes_of(gc_all, rows_of(ch), p, DN_HEADS) for ch, p in probs]
    gl2 = [lanes_of(gc_all, last_of(ch), p, DN_HEADS) for ch, p in probs]
    gcrow2 = [jnp.concatenate(
        [jnp.broadcast_to(gct_all[DN_HEADS + 2 * p + j:DN_HEADS + 2 * p + j + 1, rows_of(ch)], (CHUNK, CHUNK))
         for j in range(2)], axis=1) for ch, p in probs]
    decay2 = [jnp.where(incl, jnp.exp(a - b), 0.0) for a, b in zip(gc2, gcrow2)]
    egc2 = [jnp.exp(x) for x in gc2]
    kb2 = [x * y for x, y in zip(k2, beta2)]
    kbf = [x.astype(BF16) for x in k2]
    qbf = [q_ref[rows_of(ch), cols_of(p)].astype(BF16) for ch, p in probs]
    aq = [lax.dot_general(jnp.concatenate([x.astype(BF16), y], axis=0), _block_diag(z),
                          (((1,), (1,)), ((), ())), preferred_element_type=F32)
          for x, y, z in zip(kb2, qbf, kbf)]
    m2 = [jnp.where(strict, x[0:CHUNK] * d, 0.0) for x, d in zip(aq, decay2)]
    attn_b = [(x[CHUNK:2 * CHUNK] * d).astype(BF16) for x, d in zip(aq, decay2)]
    t_b = [x.astype(BF16) for x in _unit_lower_inverse(m2, masks)]
    u2 = [_pair_dot(t, (v_ref[rows_of(ch), cols_of(p)] * bt).astype(BF16))
          for t, bt, (ch, p) in zip(t_b, beta2, probs)]
    w2 = [_pair_dot(t, (x * e).astype(BF16)) for t, x, e in zip(t_b, kb2, egc2)]
    wqg_b = [jnp.concatenate([w.astype(BF16), (q_ref[rows_of(ch), cols_of(p)] * e).astype(BF16)], axis=0)
             for w, e, (ch, p) in zip(w2, egc2, probs)]
    kgt_b = [(x * jnp.exp(gl - gc)).T.astype(BF16)
             for x, gl, gc in zip(k2, gl2, gc2)]
    egl2 = [jnp.exp(x) for x in gl2]

    states = [state_ref[p] for p in range(N_PAIRS)]
    for ch in range(n_chunks):
        rows = rows_of(ch)
        idx = [ch * N_PAIRS + p for p in range(N_PAIRS)]
        wq = [_pair_dot(wqg_b[i], states[p].astype(BF16)) for p, i in enumerate(idx)]
        v_new_b = [(u2[i] - x[0:CHUNK]).astype(BF16) for x, i in zip(wq, idx)]
        o2 = [x[CHUNK:2 * CHUNK] + _pair_dot(attn_b[i], vn) for x, vn, i in zip(wq, v_new_b, idx)]
        full = [_dot(kgt_b[i], vn) for vn, i in zip(v_new_b, idx)]
        states = [st * egl2[i] + jnp.concatenate([f[0:LANES, 0:LANES], f[LANES:PAIR, LANES:PAIR]], axis=1)
                  for st, f, i in zip(states, full, idx)]
        ms = [_dot((x * x).astype(BF16), ones_bd) * (1.0 / DN_HEAD_DIM) for x in o2]
        for p in range(N_PAIRS):
            cols = cols_of(p)
            yb = o2[p] * lax.rsqrt(ms[p] + RMS_EPS) * onw2 * sz_ref[rows, cols]
            out_ref[rows, cols] = (yag_ref[rows, cols] + sgb_ref[rows, cols] * yb).astype(out_ref.dtype)

    for p in range(N_PAIRS):
        state_ref[p] = states[p]


def _mixer(q, k, v, sz, sgb, yag, bg, o_norm_w, batch, seq):
    ts = TILE_MIX
    n_s = seq // ts
    tok = lambda n: pl.BlockSpec((ts, n), lambda b, s: (b * n_s + s, 0))
    return pl.pallas_call(
        _mixer_body,
        grid=(batch, n_s),
        in_specs=[tok(DN_WIDTH), tok(DN_WIDTH), tok(DN_WIDTH), tok(DN_WIDTH), tok(D_MODEL), tok(D_MODEL),
                  tok(GATE_PAD), _resident((1, DN_HEAD_DIM))],
        out_specs=tok(D_MODEL),
        out_shape=jax.ShapeDtypeStruct((batch * seq, D_MODEL), BF16),
        scratch_shapes=[pltpu.VMEM((N_PAIRS, DN_HEAD_DIM, PAIR), F32)],
        compiler_params=pltpu.CompilerParams(dimension_semantics=("arbitrary", "arbitrary"),
                                             vmem_limit_bytes=VMEM_LIMIT),
        name="mixer",
    )(q, k, v, sz, sgb, yag, bg, o_norm_w)


def _ffn_body(alpha, x_ref, mixed_ref, p_ref, lng_ref, lnb_ref, wout_ref, ln1g_ref, ln1b_ref,
              wup_ref, wdown_ref, wg_ref, wp_ref, ln2g_ref, ln2b_ref, o_ref):
    h = _layer_norm(x_ref[...], lng_ref[...], lnb_ref[...])
    t = alpha * h + _dot(mixed_ref[...], wout_ref[...])
    h1 = _layer_norm(t, ln1g_ref[...], ln1b_ref[...])
    h1b = h1.astype(BF16)
    r = alpha * h1
    for c0 in range(0, D_FF, 1024):
        up = _dot(h1b, wup_ref[:, c0:c0 + 1024])
        act = jnp.square(jnp.maximum(up, 0.0)).astype(BF16)
        r = r + _dot(act, wdown_ref[c0:c0 + 1024, :])
    gate = _sigmoid(_dot(r.astype(BF16), wg_ref[...]))
    ple = gate * _dot(p_ref[...].astype(BF16), wp_ref[...])
    o_ref[...] = _layer_norm(r + ple, ln2g_ref[...], ln2b_ref[...])


def _channel_mixer(alpha, x2, mixed, p2, ln_g, ln_b, w_out, ln1_g, ln1_b, w_up, w_down, w_g, w_p, ln2_g, ln2_b):
    t = x2.shape[0]
    tm = TILE_FFN
    tok = lambda n: pl.BlockSpec((tm, n), lambda i: (i, 0))
    vec = _resident((1, D_MODEL))
    return pl.pallas_call(
        functools.partial(_ffn_body, alpha),
        grid=(t // tm,),
        in_specs=[tok(D_MODEL), tok(D_MODEL), tok(PLE_DIM), vec, vec,
                  _resident((D_MODEL, D_MODEL)), vec, vec,
                  _resident((D_MODEL, D_FF)), _resident((D_FF, D_MODEL)),
                  _resident((D_MODEL, D_MODEL)), _resident((PLE_DIM, D_MODEL)), vec, vec],
        out_specs=tok(D_MODEL),
        out_shape=jax.ShapeDtypeStruct((t, D_MODEL), F32),
        compiler_params=pltpu.CompilerParams(dimension_semantics=("arbitrary",),
                                             vmem_limit_bytes=VMEM_LIMIT),
        name="channel_mixer",
    )(x2, mixed, p2, ln_g, ln_b, w_out, ln1_g, ln1_b, w_up, w_down, w_g, w_p, ln2_g, ln2_b)


def kernel(x, p, ln_in_g, ln_in_b, w_in, pool_w, pool_scale, conv_w, a_log, dt_bias, o_norm_w, w_out,
           ln1_g, ln1_b, w_up, w_down, ple_gate_w, ple_proj_w, ln2_g, ln2_b):
    batch, seq, _ = x.shape
    depth = w_in.shape[0]
    assert depth == 1, "the fused input layer norm assumes a single layer"
    alpha = (2.0 * depth) ** 0.25
    tokens = batch * seq
    x2 = x.reshape(tokens, D_MODEL)
    row = lambda v: v.reshape(1, -1).astype(F32)

    i = 0
    o_beta = POOL_WIDTH + QKV_WIDTH + DN_WIDTH
    o_ga = o_beta + 2 * DN_HEADS
    w = w_in[i]
    w_cat = jnp.concatenate(
        [w[:, :o_beta], w[:, o_ga:], w[:, o_beta:o_ga],
         jnp.zeros((D_MODEL, GATE_PAD - 2 * DN_HEADS), w.dtype)], axis=1).astype(BF16)
    pad = jnp.zeros((GATE_PAD - 2 * DN_HEADS,), F32)
    zeros_h = jnp.zeros((DN_HEADS,), F32)
    alog_vec = jnp.concatenate([zeros_h, a_log[i].astype(F32), pad]).reshape(1, GATE_PAD)
    dtb_vec = jnp.concatenate([zeros_h, dt_bias[i].astype(F32), pad]).reshape(1, GATE_PAD)
    yag, q, k, v, sz, sgb, bg = _in_projection(
        x2, row(ln_in_g), row(ln_in_b), w_cat, pool_w[i].astype(BF16), row(pool_scale[i]),
        conv_w[i].astype(F32), alog_vec, dtb_vec, seq)

    mixed = _mixer(q, k, v, sz, sgb, yag, bg, row(o_norm_w[i]), batch, seq)

    out = _channel_mixer(alpha, x2, mixed, p[i].reshape(tokens, PLE_DIM), row(ln_in_g), row(ln_in_b),
                         w_out[i].astype(BF16), row(ln1_g[i]), row(ln1_b[i]),
                         w_up[i].astype(BF16), w_down[i].astype(BF16),
                         ple_gate_w[i].astype(BF16), ple_proj_w[i].astype(BF16),
                         row(ln2_g[i]), row(ln2_b[i]))
    return out.reshape(batch, seq, D_MODEL)
```

```python
import functools

import jax
import jax.numpy as jnp
from jax import lax
from jax.experimental import pallas as pl
from jax.experimental.pallas import tpu as pltpu

F32 = jnp.float32
BF16 = jnp.bfloat16

D_MODEL = 1024
POOL_WINDOWS = (2, 4, 8, 16)
N_POOL_GROUPS = 4
POOL_WIDTH = D_MODEL // 2
POOL_GROUP = POOL_WIDTH // N_POOL_GROUPS
POOL_OUT_GROUP = D_MODEL // N_POOL_GROUPS
DN_HEADS = 8
DN_HEAD_DIM = 128
DN_WIDTH = DN_HEADS * DN_HEAD_DIM
CONV_K = 4
D_FF = 4 * D_MODEL
PLE_DIM = 256
LN_EPS = 1e-5
RMS_EPS = 1e-6
L2_EPS = 1e-6
QKV_WIDTH = 3 * DN_WIDTH
LOG2E = 1.4426950408889634

LANES = 128
PAIR = 2 * LANES
N_PAIRS = DN_HEADS // 2
GATE_PAD = LANES
POOL_HIST = 16
CONV_HIST = 8

CHUNK = 128
TILE_PROJ = 256
TILE_MIX = 512
TILE_FFN = 512
VMEM_LIMIT = 56 * 1024 * 1024

_O_QKV = POOL_WIDTH
_O_Z = _O_QKV + QKV_WIDTH
_O_GA = _O_Z + DN_WIDTH
_O_GB = _O_GA + D_MODEL
_O_BG = _O_GB + D_MODEL
_N_PROJ = _O_BG + GATE_PAD


def _layer_norm(x, g, b):
    mu = jnp.mean(x, axis=-1, keepdims=True)
    xc = x - mu
    var = jnp.mean(xc * xc, axis=-1, keepdims=True)
    return xc * lax.rsqrt(var + LN_EPS) * g + b


def _sigmoid(x):
    return 1.0 / (1.0 + jnp.exp2(x * (-LOG2E)))


def _softplus(x):
    return jnp.maximum(x, 0.0) + jnp.log1p(jnp.exp(-jnp.abs(x)))


def _dot(a, b):
    return jnp.dot(a, b, preferred_element_type=F32)


def _resident(shape):
    return pl.BlockSpec(shape, lambda *_: (0,) * len(shape), pipeline_mode=pl.Buffered(1))


def _proj_body(tiles_per_seq, x_ref, g_ref, b_ref, w_ref, poolw_ref, pscale_ref, convw_ref, alog_ref, dtb_ref,
               yag_ref, q_ref, k_ref, v_ref, sz_ref, sgb_ref, bg_ref,
               ubuf_ref, qbuf_ref):
    tm = x_ref.shape[0]
    s = pl.program_id(0) % tiles_per_seq

    @pl.when(s == 0)
    def _():
        ubuf_ref[0:POOL_HIST, :] = jnp.zeros((POOL_HIST, POOL_WIDTH), F32)
        qbuf_ref[0:CONV_HIST, :] = jnp.zeros((CONV_HIST, QKV_WIDTH), F32)

    h = _layer_norm(x_ref[...], g_ref[...], b_ref[...]).astype(BF16)

    def proj(c0, n):
        return _dot(h, w_ref[:, c0:c0 + n])

    tpos = s * tm + lax.broadcasted_iota(jnp.int32, (tm, 1), 0)

    def pool_slice(c0):
        ubuf_ref[POOL_HIST:POOL_HIST + tm, c0:c0 + PAIR] = proj(c0, PAIR)

    def pool_group(gi):
        w = POOL_WINDOWS[gi]
        ue = ubuf_ref[0:POOL_HIST + tm, gi * POOL_GROUP:(gi + 1) * POOL_GROUP]
        win, span = ue, 1
        while span < w:
            win = win + pltpu.roll(win, span, axis=0)
            span *= 2
        cnt = jnp.minimum(tpos + 1, w).astype(F32)
        d = win[POOL_HIST:POOL_HIST + tm] / cnt - ue[POOL_HIST:POOL_HIST + tm]
        ocols = slice(gi * POOL_OUT_GROUP, (gi + 1) * POOL_OUT_GROUP)
        ya = _dot(d.astype(BF16), poolw_ref[gi]) * pscale_ref[:, ocols]
        ga = proj(_O_GA + gi * POOL_OUT_GROUP, POOL_OUT_GROUP)
        yag_ref[:, ocols] = _sigmoid(ga) * ya

    def gate_slice(ref, o0, c0, is_silu):
        pre = proj(o0 + c0, PAIR)
        act = _sigmoid(pre)
        ref[:, c0:c0 + PAIR] = pre * act if is_silu else act

    for c0 in range(0, POOL_WIDTH, PAIR):
        pool_slice(c0)
    for gi in range(N_POOL_GROUPS):
        pool_group(gi)

    side = ([functools.partial(gate_slice, sz_ref, _O_Z, c0, True) for c0 in range(0, DN_WIDTH, PAIR)]
            + [functools.partial(gate_slice, sgb_ref, _O_GB, c0, False) for c0 in range(0, D_MODEL, PAIR)])
    n_groups = QKV_WIDTH // LANES
    n_slices = QKV_WIDTH // PAIR

    def issue_slice(sl):
        qbuf_ref[CONV_HIST:CONV_HIST + tm, sl * PAIR:(sl + 1) * PAIR] = proj(_O_QKV + sl * PAIR, PAIR)
        for unit in side[sl * len(side) // n_slices:(sl + 1) * len(side) // n_slices]:
            unit()

    for grp in range(n_groups):
        if grp % 2 == 0:
            issue_slice(grp // 2)
        cols = slice(grp * LANES, (grp + 1) * LANES)
        xe = qbuf_ref[0:CONV_HIST + tm, cols]
        acc = None
        for k in range(CONV_K):
            back = CONV_K - 1 - k
            xs = xe if back == 0 else pltpu.roll(xe, back, axis=0)
            term = xs[CONV_HIST:CONV_HIST + tm] * convw_ref[k:k + 1, cols]
            acc = term if acc is None else acc + term
        y = acc * _sigmoid(acc)
        hcols = slice((grp % DN_HEADS) * LANES, (grp % DN_HEADS + 1) * LANES)
        if grp < DN_HEADS:
            q_ref[:, hcols] = y * (lax.rsqrt(jnp.sum(y * y, axis=-1, keepdims=True) + L2_EPS)
                                   * (DN_HEAD_DIM ** -0.5))
        elif grp < 2 * DN_HEADS:
            k_ref[:, hcols] = y * lax.rsqrt(jnp.sum(y * y, axis=-1, keepdims=True) + L2_EPS)
        else:
            v_ref[:, hcols] = y
    qbuf_ref[0:CONV_HIST, :] = qbuf_ref[tm:tm + CONV_HIST, :]
    ubuf_ref[0:POOL_HIST, :] = ubuf_ref[tm:tm + POOL_HIST, :]

    ba = proj(_O_BG, GATE_PAD)
    lane = lax.broadcasted_iota(jnp.int32, (1, GATE_PAD), 1)
    is_decay = (lane >= DN_HEADS) & (lane < 2 * DN_HEADS)
    a_scale = jnp.where(is_decay, jnp.exp(alog_ref[...]), 0.0)
    g = -a_scale * _softplus(ba + dtb_ref[...])
    bg_ref[...] = jnp.where(lane < DN_HEADS, _sigmoid(ba), g)


def _in_projection(x2, ln_g, ln_b, w_cat, pool_w, pool_scale, conv_w, alog_vec, dtb_vec, seq):
    t = x2.shape[0]
    tm = TILE_PROJ
    tok = lambda n: pl.BlockSpec((tm, n), lambda i: (i, 0))
    widths = (D_MODEL, DN_WIDTH, DN_WIDTH, DN_WIDTH, DN_WIDTH, D_MODEL, GATE_PAD)
    return pl.pallas_call(
        functools.partial(_proj_body, seq // tm),
        grid=(t // tm,),
        in_specs=[tok(D_MODEL), _resident((1, D_MODEL)), _resident((1, D_MODEL)),
                  _resident((D_MODEL, _N_PROJ)),
                  _resident((N_POOL_GROUPS, POOL_GROUP, POOL_OUT_GROUP)), _resident((1, D_MODEL)),
                  _resident((CONV_K, QKV_WIDTH)), _resident((1, GATE_PAD)), _resident((1, GATE_PAD))],
        out_specs=[tok(n) for n in widths],
        out_shape=[jax.ShapeDtypeStruct((t, n), F32) for n in widths],
        scratch_shapes=[pltpu.VMEM((POOL_HIST + tm, POOL_WIDTH), F32),
                        pltpu.VMEM((CONV_HIST + tm, QKV_WIDTH), F32)],
        compiler_params=pltpu.CompilerParams(dimension_semantics=("arbitrary",),
                                             vmem_limit_bytes=VMEM_LIMIT),
        name="in_projection",
    )(x2, ln_g, ln_b, w_cat, pool_w, pool_scale, conv_w, alog_vec, dtb_vec)


def _block_diag(a2):
    z = jnp.zeros((a2.shape[0], LANES), a2.dtype)
    return jnp.concatenate([jnp.concatenate([a2[:, :LANES], z], axis=1),
                            jnp.concatenate([z, a2[:, LANES:]], axis=1)], axis=0)


def _pair_dot(a2, b2):
    return _dot(a2, _block_diag(b2))


def _unit_lower_inverse(ms, masks):
    eye, base_mask, level_masks = masks
    n = [jnp.where(base_mask, m, 0.0) for m in ms]
    p = [eye - x for x in n]
    nb = [x.astype(BF16) for x in n]
    qb = [_pair_dot(x, x).astype(BF16) for x in nb]
    p = [x + _pair_dot(x.astype(BF16), y) for x, y in zip(p, qb)]
    qb = [_pair_dot(x, x).astype(BF16) for x in qb]
    t = [x + _pair_dot(x.astype(BF16), y) for x, y in zip(p, qb)]
    for lm in level_masks:
        cb = [jnp.where(lm, m, 0.0).astype(BF16) for m in ms]
        tb = [x.astype(BF16) for x in t]
        yb = [_pair_dot(c, x).astype(BF16) for c, x in zip(cb, tb)]
        t = [x - _pair_dot(xb, y) for x, xb, y in zip(t, tb, yb)]
    return t


def _mixer_body(q_ref, k_ref, v_ref, sz_ref, sgb_ref, yag_ref, bg_ref, onw_ref, out_ref, state_ref):
    ts = out_ref.shape[0]
    n_chunks = ts // CHUNK

    @pl.when(pl.program_id(1) == 0)
    def _():
        state_ref[...] = jnp.zeros_like(state_ref)

    bg = bg_ref[...]
    b_hi = bg.astype(BF16)
    r1 = bg - b_hi.astype(F32)
    b_mid = r1.astype(BF16)
    b_lo = (r1 - b_mid.astype(F32)).astype(BF16)
    ri = lax.broadcasted_iota(jnp.int32, (ts, ts), 0)
    ci = lax.broadcasted_iota(jnp.int32, (ts, ts), 1)
    tri = jnp.where(((ri // CHUNK) == (ci // CHUNK)) & (ri >= ci), 1.0, 0.0).astype(BF16)
    gc_all = _dot(tri, b_hi) + _dot(tri, b_mid) + _dot(tri, b_lo)
    gct_all = gc_all.T

    r = lax.broadcasted_iota(jnp.int32, (CHUNK, PAIR), 0)
    c = lax.broadcasted_iota(jnp.int32, (CHUNK, PAIR), 1) % LANES
    incl = r >= c
    strict = r > c
    eye = jnp.where(r == c, 1.0, 0.0).astype(F32)
    base_mask = strict & ((r // 8) == (c // 8))
    level_masks = []
    b = 8
    while b < CHUNK:
        level_masks.append(strict & ((r // (2 * b)) == (c // (2 * b))) & ((r // b) != (c // b)))
        b *= 2
    masks = (eye, base_mask, level_masks)
    ones_bd = _block_diag(jnp.ones((LANES, PAIR), BF16))
    onw2 = jnp.concatenate([onw_ref[...], onw_ref[...]], axis=1)

    def lanes_of(arr, rows, p, lane0):
        return jnp.concatenate(
            [jnp.broadcast_to(arr[rows, lane0 + 2 * p + j:lane0 + 2 * p + j + 1], (CHUNK, LANES))
             for j in range(2)], axis=1)

    probs = [(ch, p) for ch in range(n_chunks) for p in range(N_PAIRS)]
    rows_of = lambda ch: slice(ch * CHUNK, (ch + 1) * CHUNK)
    last_of = lambda ch: slice((ch + 1) * CHUNK - 1, (ch + 1) * CHUNK)
    cols_of = lambda p: slice(p * PAIR, (p + 1) * PAIR)

    k2 = [k_ref[rows_of(ch), cols_of(p)] for ch, p in probs]
    beta2 = [lanes_of(bg, rows_of(ch), p, 0) for ch, p in probs]
    gc2 = [lanes_of(gc_all, rows_of(ch), p, DN_HEADS) for ch, p in probs]
    gl2 = [lanes_of(gc_all, last_of(ch), p, DN_HEADS) for ch, p in probs]
    gcrow2 = [jnp.concatenate(
        [jnp.broadcast_to(gct_all[DN_HEADS + 2 * p + j:DN_HEADS + 2 * p + j + 1, rows_of(ch)], (CHUNK, CHUNK))
         for j in range(2)], axis=1) for ch, p in probs]
    decay2 = [jnp.where(incl, jnp.exp(a - b), 0.0) for a, b in zip(gc2, gcrow2)]
    egc2 = [jnp.exp(x) for x in gc2]
    kb2 = [x * y for x, y in zip(k2, beta2)]
    kbf = [x.astype(BF16) for x in k2]
    qbf = [q_ref[rows_of(ch), cols_of(p)].astype(BF16) for ch, p in probs]
    aq = [lax.dot_general(jnp.concatenate([x.astype(BF16), y], axis=0), _block_diag(z),
                          (((1,), (1,)), ((), ())), preferred_element_type=F32)
          for x, y, z in zip(kb2, qbf, kbf)]
    m2 = [jnp.where(strict, x[0:CHUNK] * d, 0.0) for x, d in zip(aq, decay2)]
    attn_b = [(x[CHUNK:2 * CHUNK] * d).astype(BF16) for x, d in zip(aq, decay2)]
    t_b = [x.astype(BF16) for x in _unit_lower_inverse(m2, masks)]
    u2 = [_pair_dot(t, (v_ref[rows_of(ch), cols_of(p)] * bt).astype(BF16))
          for t, bt, (ch, p) in zip(t_b, beta2, probs)]
    w2 = [_pair_dot(t, (x * e).astype(BF16)) for t, x, e in zip(t_b, kb2, egc2)]
    wqg_b = [jnp.concatenate([w.astype(BF16), (q_ref[rows_of(ch), cols_of(p)] * e).astype(BF16)], axis=0)
             for w, e, (ch, p) in zip(w2, egc2, probs)]
    kgt_b = [(x * jnp.exp(gl - gc)).T.astype(BF16)
             for x, gl, gc in zip(k2, gl2, gc2)]
    egl2 = [jnp.exp(x) for x in gl2]

    states = [state_ref[p] for p in range(N_PAIRS)]
    for ch in range(n_chunks):
        rows = rows_of(ch)
        idx = [ch * N_PAIRS + p for p in range(N_PAIRS)]
        wq = [_pair_dot(wqg_b[i], states[p].astype(BF16)) for p, i in enumerate(idx)]
        v_new_b = [(u2[i] - x[0:CHUNK]).astype(BF16) for x, i in zip(wq, idx)]
        o2 = [x[CHUNK:2 * CHUNK] + _pair_dot(attn_b[i], vn) for x, vn, i in zip(wq, v_new_b, idx)]
        full = [_dot(kgt_b[i], vn) for vn, i in zip(v_new_b, idx)]
        states = [st * egl2[i] + jnp.concatenate([f[0:LANES, 0:LANES], f[LANES:PAIR, LANES:PAIR]], axis=1)
                  for st, f, i in zip(states, full, idx)]
        ms = [_dot((x * x).astype(BF16), ones_bd) * (1.0 / DN_HEAD_DIM) for x in o2]
        for p in range(N_PAIRS):
            cols = cols_of(p)
            yb = o2[p] * lax.rsqrt(ms[p] + RMS_EPS) * onw2 * sz_ref[rows, cols]
            out_ref[rows, cols] = (yag_ref[rows, cols] + sgb_ref[rows, cols] * yb).astype(out_ref.dtype)

    for p in range(N_PAIRS):
        state_ref[p] = states[p]


def _mixer(q, k, v, sz, sgb, yag, bg, o_norm_w, batch, seq):
    ts = TILE_MIX
    n_s = seq // ts
    tok = lambda n: pl.BlockSpec((ts, n), lambda b, s: (b * n_s + s, 0))
    return pl.pallas_call(
        _mixer_body,
        grid=(batch, n_s),
        in_specs=[tok(DN_WIDTH), tok(DN_WIDTH), tok(DN_WIDTH), tok(DN_WIDTH), tok(D_MODEL), tok(D_MODEL),
                  tok(GATE_PAD), _resident((1, DN_HEAD_DIM))],
        out_specs=tok(D_MODEL),
        out_shape=jax.ShapeDtypeStruct((batch * seq, D_MODEL), BF16),
        scratch_shapes=[pltpu.VMEM((N_PAIRS, DN_HEAD_DIM, PAIR), F32)],
        compiler_params=pltpu.CompilerParams(dimension_semantics=("arbitrary", "arbitrary"),
                                             vmem_limit_bytes=VMEM_LIMIT),
        name="mixer",
    )(q, k, v, sz, sgb, yag, bg, o_norm_w)


def _ffn_body(alpha, x_ref, mixed_ref, p_ref, lng_ref, lnb_ref, wout_ref, ln1g_ref, ln1b_ref,
              wup_ref, wdown_ref, wg_ref, wp_ref, ln2g_ref, ln2b_ref, o_ref):
    half = x_ref.shape[0] // 2
    n_ff = D_FF // 1024

    def prologue(rows):
        h = _layer_norm(x_ref[rows, :], lng_ref[...], lnb_ref[...])
        t = alpha * h + _dot(mixed_ref[rows, :], wout_ref[...])
        h1 = _layer_norm(t, ln1g_ref[...], ln1b_ref[...])
        return h1.astype(BF16), alpha * h1

    def mlp_chunk(h1b, r, c):
        up = _dot(h1b, wup_ref[:, c * 1024:(c + 1) * 1024])
        act = jnp.square(jnp.maximum(up, 0.0)).astype(BF16)
        return r + _dot(act, wdown_ref[c * 1024:(c + 1) * 1024, :])

    def epilogue(r, rows):
        gate = _sigmoid(_dot(r.astype(BF16), wg_ref[...]))
        ple = gate * _dot(p_ref[rows, :].astype(BF16), wp_ref[...])
        o_ref[rows, :] = _layer_norm(r + ple, ln2g_ref[...], ln2b_ref[...])

    rows_a, rows_b = slice(0, half), slice(half, 2 * half)
    h1b_a, r_a = prologue(rows_a)
    r_a = mlp_chunk(h1b_a, r_a, 0)
    h1b_b, r_b = prologue(rows_b)
    for c in range(1, n_ff):
        r_a = mlp_chunk(h1b_a, r_a, c)
    r_b = mlp_chunk(h1b_b, r_b, 0)
    epilogue(r_a, rows_a)
    for c in range(1, n_ff):
        r_b = mlp_chunk(h1b_b, r_b, c)
    epilogue(r_b, rows_b)


def _channel_mixer(alpha, x2, mixed, p2, ln_g, ln_b, w_out, ln1_g, ln1_b, w_up, w_down, w_g, w_p, ln2_g, ln2_b):
    t = x2.shape[0]
    tm = TILE_FFN
    tok = lambda n: pl.BlockSpec((tm, n), lambda i: (i, 0))
    vec = _resident((1, D_MODEL))
    return pl.pallas_call(
        functools.partial(_ffn_body, alpha),
        grid=(t // tm,),
        in_specs=[tok(D_MODEL), tok(D_MODEL), tok(PLE_DIM), vec, vec,
                  _resident((D_MODEL, D_MODEL)), vec, vec,
                  _resident((D_MODEL, D_FF)), _resident((D_FF, D_MODEL)),
                  _resident((D_MODEL, D_MODEL)), _resident((PLE_DIM, D_MODEL)), vec, vec],
        out_specs=tok(D_MODEL),
        out_shape=jax.ShapeDtypeStruct((t, D_MODEL), F32),
        compiler_params=pltpu.CompilerParams(dimension_semantics=("arbitrary",),
                                             vmem_limit_bytes=VMEM_LIMIT),
        name="channel_mixer",
    )(x2, mixed, p2, ln_g, ln_b, w_out, ln1_g, ln1_b, w_up, w_down, w_g, w_p, ln2_g, ln2_b)


def kernel(x, p, ln_in_g, ln_in_b, w_in, pool_w, pool_scale, conv_w, a_log, dt_bias, o_norm_w, w_out,
           ln1_g, ln1_b, w_up, w_down, ple_gate_w, ple_proj_w, ln2_g, ln2_b):
    batch, seq, _ = x.shape
    depth = w_in.shape[0]
    assert depth == 1, "the fused input layer norm assumes a single layer"
    alpha = (2.0 * depth) ** 0.25
    tokens = batch * seq
    x2 = x.reshape(tokens, D_MODEL)
    row = lambda v: v.reshape(1, -1).astype(F32)

    i = 0
    o_beta = POOL_WIDTH + QKV_WIDTH + DN_WIDTH
    o_ga = o_beta + 2 * DN_HEADS
    w = w_in[i].astype(BF16)
    w_cat = jnp.concatenate(
        [w[:, :o_beta], w[:, o_ga:], w[:, o_beta:o_ga],
         jnp.zeros((D_MODEL, GATE_PAD - 2 * DN_HEADS), BF16)], axis=1)
    pad = jnp.zeros((GATE_PAD - 2 * DN_HEADS,), F32)
    zeros_h = jnp.zeros((DN_HEADS,), F32)
    alog_vec = jnp.concatenate([zeros_h, a_log[i].astype(F32), pad]).reshape(1, GATE_PAD)
    dtb_vec = jnp.concatenate([zeros_h, dt_bias[i].astype(F32), pad]).reshape(1, GATE_PAD)
    yag, q, k, v, sz, sgb, bg = _in_projection(
        x2, row(ln_in_g), row(ln_in_b), w_cat, pool_w[i].astype(BF16), row(pool_scale[i]),
        conv_w[i].astype(F32), alog_vec, dtb_vec, seq)

    mixed = _mixer(q, k, v, sz, sgb, yag, bg, row(o_norm_w[i]), batch, seq)

    out = _channel_mixer(alpha, x2, mixed, p[i].reshape(tokens, PLE_DIM), row(ln_in_g), row(ln_in_b),
                         w_out[i].astype(BF16), row(ln1_g[i]), row(ln1_b[i]),
                         w_up[i].astype(BF16), w_down[i].astype(BF16),
                         ple_gate_w[i].astype(BF16), ple_proj_w[i].astype(BF16),
                         row(ln2_g[i]), row(ln2_b[i]))
    return out.reshape(batch, seq, D_MODEL)
```

```python
import functools

import jax
import jax.numpy as jnp
from jax import lax
from jax.experimental import pallas as pl
from jax.experimental.pallas import tpu as pltpu

F32 = jnp.float32
BF16 = jnp.bfloat16

D_MODEL = 1024
POOL_WINDOWS = (2, 4, 8, 16)
N_POOL_GROUPS = 4
POOL_WIDTH = D_MODEL // 2
POOL_GROUP = POOL_WIDTH // N_POOL_GROUPS
POOL_OUT_GROUP = D_MODEL // N_POOL_GROUPS
DN_HEADS = 8
DN_HEAD_DIM = 128
DN_WIDTH = DN_HEADS * DN_HEAD_DIM
CONV_K = 4
D_FF = 4 * D_MODEL
PLE_DIM = 256
LN_EPS = 1e-5
RMS_EPS = 1e-6
L2_EPS = 1e-6
QKV_WIDTH = 3 * DN_WIDTH
LOG2E = 1.4426950408889634

LANES = 128
PAIR = 2 * LANES
N_PAIRS = DN_HEADS // 2
GATE_PAD = LANES
POOL_HIST = 16
CONV_HIST = 8

CHUNK = 128
TILE_PROJ = 256
TILE_MIX = 256
TILE_FFN = 512
VMEM_LIMIT = 56 * 1024 * 1024

_O_QKV = POOL_WIDTH
_O_Z = _O_QKV + QKV_WIDTH
_O_GA = _O_Z + DN_WIDTH
_O_GB = _O_GA + D_MODEL
_O_BG = _O_GB + D_MODEL
_N_PROJ = _O_BG + GATE_PAD


def _layer_norm(x, g, b):
    mu = jnp.mean(x, axis=-1, keepdims=True)
    xc = x - mu
    var = jnp.mean(xc * xc, axis=-1, keepdims=True)
    return xc * lax.rsqrt(var + LN_EPS) * g + b


def _sigmoid(x):
    return 1.0 / (1.0 + jnp.exp2(x * (-LOG2E)))


def _softplus(x):
    return jnp.maximum(x, 0.0) + jnp.log1p(jnp.exp(-jnp.abs(x)))


def _dot(a, b):
    return jnp.dot(a, b, preferred_element_type=F32)


def _resident(shape):
    return pl.BlockSpec(shape, lambda *_: (0,) * len(shape), pipeline_mode=pl.Buffered(1))


def _proj_body(tiles_per_seq, x_ref, g_ref, b_ref, w_ref, wg_ref, poolw_ref, pscale_ref, convw_ref, alog_ref,
               dtb_ref, yag_ref, q_ref, k_ref, v_ref, sz_ref, sgb_ref, bg_ref,
               ubuf_ref, qbuf_ref):
    tm = x_ref.shape[0]
    s = pl.program_id(0) % tiles_per_seq

    @pl.when(s == 0)
    def _():
        ubuf_ref[0:POOL_HIST, :] = jnp.zeros((POOL_HIST, POOL_WIDTH), F32)
        qbuf_ref[0:CONV_HIST, :] = jnp.zeros((CONV_HIST, QKV_WIDTH), F32)

    h = _layer_norm(x_ref[...], g_ref[...], b_ref[...]).astype(BF16)

    def proj(c0, n):
        if c0 < _O_GA:
            return _dot(h, w_ref[:, c0:c0 + n])
        return _dot(h, wg_ref[:, c0 - _O_GA:c0 - _O_GA + n])

    tpos = s * tm + lax.broadcasted_iota(jnp.int32, (tm, 1), 0)

    def pool_slice(c0):
        ubuf_ref[POOL_HIST:POOL_HIST + tm, c0:c0 + PAIR] = proj(c0, PAIR)

    def pool_group(gi):
        w = POOL_WINDOWS[gi]
        ue = ubuf_ref[0:POOL_HIST + tm, gi * POOL_GROUP:(gi + 1) * POOL_GROUP]
        win, span = ue, 1
        while span < w:
            win = win + pltpu.roll(win, span, axis=0)
            span *= 2
        cnt = jnp.minimum(tpos + 1, w).astype(F32)
        d = win[POOL_HIST:POOL_HIST + tm] / cnt - ue[POOL_HIST:POOL_HIST + tm]
        ocols = slice(gi * POOL_OUT_GROUP, (gi + 1) * POOL_OUT_GROUP)
        ya = _dot(d.astype(BF16), poolw_ref[gi]) * pscale_ref[:, ocols]
        ga = proj(_O_GA + gi * POOL_OUT_GROUP, POOL_OUT_GROUP)
        yag_ref[:, ocols] = _sigmoid(ga) * ya

    def gate_slice(ref, o0, c0, is_silu):
        pre = proj(o0 + c0, PAIR)
        act = _sigmoid(pre)
        ref[:, c0:c0 + PAIR] = pre * act if is_silu else act

    for c0 in range(0, POOL_WIDTH, PAIR):
        pool_slice(c0)
    for gi in range(N_POOL_GROUPS):
        pool_group(gi)

    side = ([functools.partial(gate_slice, sz_ref, _O_Z, c0, True) for c0 in range(0, DN_WIDTH, PAIR)]
            + [functools.partial(gate_slice, sgb_ref, _O_GB, c0, False) for c0 in range(0, D_MODEL, PAIR)])
    n_groups = QKV_WIDTH // LANES
    n_slices = QKV_WIDTH // PAIR

    def issue_slice(sl):
        qbuf_ref[CONV_HIST:CONV_HIST + tm, sl * PAIR:(sl + 1) * PAIR] = proj(_O_QKV + sl * PAIR, PAIR)
        for unit in side[sl * len(side) // n_slices:(sl + 1) * len(side) // n_slices]:
            unit()

    for grp in range(n_groups):
        if grp % 2 == 0:
            issue_slice(grp // 2)
        cols = slice(grp * LANES, (grp + 1) * LANES)
        xe = qbuf_ref[0:CONV_HIST + tm, cols]
        acc = None
        for k in range(CONV_K):
            back = CONV_K - 1 - k
            xs = xe if back == 0 else pltpu.roll(xe, back, axis=0)
            term = xs[CONV_HIST:CONV_HIST + tm] * convw_ref[k:k + 1, cols]
            acc = term if acc is None else acc + term
        y = acc * _sigmoid(acc)
        hcols = slice((grp % DN_HEADS) * LANES, (grp % DN_HEADS + 1) * LANES)
        if grp < DN_HEADS:
            q_ref[:, hcols] = y * (lax.rsqrt(jnp.sum(y * y, axis=-1, keepdims=True) + L2_EPS)
                                   * (DN_HEAD_DIM ** -0.5))
        elif grp < 2 * DN_HEADS:
            k_ref[:, hcols] = y * lax.rsqrt(jnp.sum(y * y, axis=-1, keepdims=True) + L2_EPS)
        else:
            v_ref[:, hcols] = y
    qbuf_ref[0:CONV_HIST, :] = qbuf_ref[tm:tm + CONV_HIST, :]
    ubuf_ref[0:POOL_HIST, :] = ubuf_ref[tm:tm + POOL_HIST, :]

    ba = proj(_O_BG, GATE_PAD)
    lane = lax.broadcasted_iota(jnp.int32, (1, GATE_PAD), 1)
    is_decay = (lane >= DN_HEADS) & (lane < 2 * DN_HEADS)
    a_scale = jnp.where(is_decay, jnp.exp(alog_ref[...]), 0.0)
    g = -a_scale * _softplus(ba + dtb_ref[...])
    bg_ref[...] = jnp.where(lane < DN_HEADS, _sigmoid(ba), g)


def _in_projection(x2, ln_g, ln_b, w_main, w_gates, pool_w, pool_scale, conv_w, alog_vec, dtb_vec, seq):
    t = x2.shape[0]
    tm = TILE_PROJ
    tok = lambda n: pl.BlockSpec((tm, n), lambda i: (i, 0))
    widths = (D_MODEL, DN_WIDTH, DN_WIDTH, DN_WIDTH, DN_WIDTH, D_MODEL, GATE_PAD)
    return pl.pallas_call(
        functools.partial(_proj_body, seq // tm),
        grid=(t // tm,),
        in_specs=[tok(D_MODEL), _resident((1, D_MODEL)), _resident((1, D_MODEL)),
                  _resident((D_MODEL, _O_GA)), _resident((D_MODEL, _N_PROJ - _O_GA)),
                  _resident((N_POOL_GROUPS, POOL_GROUP, POOL_OUT_GROUP)), _resident((1, D_MODEL)),
                  _resident((CONV_K, QKV_WIDTH)), _resident((1, GATE_PAD)), _resident((1, GATE_PAD))],
        out_specs=[tok(n) for n in widths],
        out_shape=[jax.ShapeDtypeStruct((t, n), F32) for n in widths],
        scratch_shapes=[pltpu.VMEM((POOL_HIST + tm, POOL_WIDTH), F32),
                        pltpu.VMEM((CONV_HIST + tm, QKV_WIDTH), F32)],
        compiler_params=pltpu.CompilerParams(dimension_semantics=("arbitrary",),
                                             vmem_limit_bytes=VMEM_LIMIT),
        name="in_projection",
    )(x2, ln_g, ln_b, w_main, w_gates, pool_w, pool_scale, conv_w, alog_vec, dtb_vec)


def _block_diag(a2):
    z = jnp.zeros((a2.shape[0], LANES), a2.dtype)
    return jnp.concatenate([jnp.concatenate([a2[:, :LANES], z], axis=1),
                            jnp.concatenate([z, a2[:, LANES:]], axis=1)], axis=0)


def _pair_dot(a2, b2):
    return _dot(a2, _block_diag(b2))


def _unit_lower_inverse(ms, masks):
    eye, base_mask, level_masks = masks
    n = [jnp.where(base_mask, m, 0.0) for m in ms]
    p = [eye - x for x in n]
    nb = [x.astype(BF16) for x in n]
    qb = [_pair_dot(x, x).astype(BF16) for x in nb]
    yield
    p = [x + _pair_dot(x.astype(BF16), y) for x, y in zip(p, qb)]
    qb = [_pair_dot(x, x).astype(BF16) for x in qb]
    yield
    t = [x + _pair_dot(x.astype(BF16), y) for x, y in zip(p, qb)]
    yield
    for lm in level_masks:
        cb = [jnp.where(lm, m, 0.0).astype(BF16) for m in ms]
        tb = [x.astype(BF16) for x in t]
        yb = [_pair_dot(c, x).astype(BF16) for c, x in zip(cb, tb)]
        yield
        t = [x - _pair_dot(xb, y) for x, xb, y in zip(t, tb, yb)]
        yield
    return t


def _interleave(gens):
    gens = list(gens)
    while gens:
        for g in list(gens):
            try:
                next(g)
            except StopIteration:
                gens.remove(g)


def _mixer_body(q_ref, k_ref, v_ref, sz_ref, sgb_ref, yag_ref, bg_ref, onw_ref, out_ref, state_ref):
    n_seq, ts = out_ref.shape[0], out_ref.shape[1]
    n_chunks = ts // CHUNK
    units = [(b, ch) for b in range(n_seq) for ch in range(n_chunks)]

    @pl.when(pl.program_id(0) == 0)
    def _():
        state_ref[...] = jnp.zeros_like(state_ref)

    bg = [bg_ref[b] for b in range(n_seq)]
    b_hi = [x.astype(BF16) for x in bg]
    r1 = [x - y.astype(F32) for x, y in zip(bg, b_hi)]
    b_mid = [x.astype(BF16) for x in r1]
    b_lo = [(x - y.astype(F32)).astype(BF16) for x, y in zip(r1, b_mid)]
    ri = lax.broadcasted_iota(jnp.int32, (CHUNK, CHUNK), 0)
    ci = lax.broadcasted_iota(jnp.int32, (CHUNK, CHUNK), 1)
    tri = jnp.where(ri >= ci, 1.0, 0.0).astype(BF16)
    by_unit = lambda a: jnp.concatenate([a[b][ch * CHUNK:(ch + 1) * CHUNK] for b, ch in units], axis=1)
    gc_wide = (_dot(tri, by_unit(b_hi)) + _dot(tri, by_unit(b_mid))
               + _dot(tri, by_unit(b_lo)))
    gc_of = {u: gc_wide[:, i * LANES:(i + 1) * LANES] for i, u in enumerate(units)}
    gct_of = {u: g.T for u, g in gc_of.items()}

    r = lax.broadcasted_iota(jnp.int32, (CHUNK, PAIR), 0)
    c = lax.broadcasted_iota(jnp.int32, (CHUNK, PAIR), 1) % LANES
    incl = r >= c
    strict = r > c
    eye = jnp.where(r == c, 1.0, 0.0).astype(F32)
    base_mask = strict & ((r // 8) == (c // 8))
    level_masks = []
    b = 8
    while b < CHUNK:
        level_masks.append(strict & ((r // (2 * b)) == (c // (2 * b))) & ((r // b) != (c // b)))
        b *= 2
    masks = (eye, base_mask, level_masks)
    onw2 = jnp.concatenate([onw_ref[...], onw_ref[...]], axis=1)

    def head_rms_scale(x2):
        parts = []
        for j in range(2):
            xh = x2[:, j * LANES:(j + 1) * LANES]
            ms = jnp.sum(xh * xh, axis=-1, keepdims=True) * (1.0 / DN_HEAD_DIM)
            parts.append(jnp.broadcast_to(lax.rsqrt(ms + RMS_EPS), (CHUNK, LANES)))
        return jnp.concatenate(parts, axis=1)

    def lanes_of(arr, rows, p, lane0):
        return jnp.concatenate(
            [jnp.broadcast_to(arr[rows, lane0 + 2 * p + j:lane0 + 2 * p + j + 1], (CHUNK, LANES))
             for j in range(2)], axis=1)

    rows_of = lambda ch: slice(ch * CHUNK, (ch + 1) * CHUNK)
    cols_of = lambda p: slice(p * PAIR, (p + 1) * PAIR)
    ready = {}

    def prepare():
        probs = [(b, ch, p) for b, ch in units for p in range(N_PAIRS)]
        k2 = [k_ref[b, rows_of(ch), cols_of(p)] for b, ch, p in probs]
        beta2 = [lanes_of(bg[b], rows_of(ch), p, 0) for b, ch, p in probs]
        gc2 = [lanes_of(gc_of[b, ch], slice(0, CHUNK), p, DN_HEADS) for b, ch, p in probs]
        gl2 = [lanes_of(gc_of[b, ch], slice(CHUNK - 1, CHUNK), p, DN_HEADS) for b, ch, p in probs]
        gcrow2 = [jnp.concatenate(
            [jnp.broadcast_to(gct_of[b, ch][DN_HEADS + 2 * p + j:DN_HEADS + 2 * p + j + 1, :],
                              (CHUNK, CHUNK)) for j in range(2)], axis=1) for b, ch, p in probs]
        decay2 = [jnp.where(incl, jnp.exp(a - b), 0.0) for a, b in zip(gc2, gcrow2)]
        egc2 = [jnp.exp(x) for x in gc2]
        kb2 = [x * y for x, y in zip(k2, beta2)]
        kbf = [x.astype(BF16) for x in k2]
        qbf = [q_ref[b, rows_of(ch), cols_of(p)].astype(BF16) for b, ch, p in probs]
        aq = [lax.dot_general(jnp.concatenate([x.astype(BF16), y], axis=0), _block_diag(z),
                              (((1,), (1,)), ((), ())), preferred_element_type=F32)
              for x, y, z in zip(kb2, qbf, kbf)]
        yield
        m2 = [jnp.where(strict, x[0:CHUNK] * d, 0.0) for x, d in zip(aq, decay2)]
        attn_b = [(x[CHUNK:2 * CHUNK] * d).astype(BF16) for x, d in zip(aq, decay2)]
        t2 = yield from _unit_lower_inverse(m2, masks)
        t_b = [x.astype(BF16) for x in t2]
        u2 = [_pair_dot(t, (v_ref[b, rows_of(ch), cols_of(p)] * bt).astype(BF16))
              for t, bt, (b, ch, p) in zip(t_b, beta2, probs)]
        w2 = [_pair_dot(t, (x * e).astype(BF16)) for t, x, e in zip(t_b, kb2, egc2)]
        yield
        wqg_b = [jnp.concatenate([w.astype(BF16), (q_ref[b, rows_of(ch), cols_of(p)] * e).astype(BF16)], axis=0)
                 for w, e, (b, ch, p) in zip(w2, egc2, probs)]
        kgt_b = [(x * jnp.exp(gl - gc)).T.astype(BF16)
                 for x, gl, gc in zip(k2, gl2, gc2)]
        egl2 = [jnp.exp(x) for x in gl2]
        for i, key in enumerate(probs):
            ready[key] = (u2[i], wqg_b[i], attn_b[i], kgt_b[i], egl2[i])
        yield

    lanes = [(b, p) for b in range(n_seq) for p in range(N_PAIRS)]
    states = [state_ref[b * N_PAIRS + p] for b, p in lanes]

    def recur():
        for ch in range(n_chunks):
            rows = rows_of(ch)
            u2, wqg_b, attn_b, kgt_b, egl2 = zip(*[ready[(b, ch, p)] for b, p in lanes])
            wq = [_pair_dot(a, st.astype(BF16)) for a, st in zip(wqg_b, states)]
            yield
            v_new_b = [(u - x[0:CHUNK]).astype(BF16) for u, x in zip(u2, wq)]
            o2 = [x[CHUNK:2 * CHUNK] + _pair_dot(a, vn) for x, a, vn in zip(wq, attn_b, v_new_b)]
            full = [_dot(kt, vn) for kt, vn in zip(kgt_b, v_new_b)]
            yield
            for i, f in enumerate(full):
                states[i] = states[i] * egl2[i] + jnp.concatenate(
                    [f[0:LANES, 0:LANES], f[LANES:PAIR, LANES:PAIR]], axis=1)
            yield
            for i, (b, p) in enumerate(lanes):
                cols = cols_of(p)
                yb = o2[i] * head_rms_scale(o2[i]) * onw2 * sz_ref[b, rows, cols]
                out_ref[b, rows, cols] = (yag_ref[b, rows, cols]
                                          + sgb_ref[b, rows, cols] * yb).astype(out_ref.dtype)

    _interleave([prepare()])
    _interleave([recur()])

    for i, (b, p) in enumerate(lanes):
        state_ref[b * N_PAIRS + p] = states[i]


def _mixer(q, k, v, sz, sgb, yag, bg, o_norm_w, batch, seq):
    ts = TILE_MIX
    tok = lambda n: pl.BlockSpec((batch, ts, n), lambda s: (0, s, 0))
    per_seq = lambda a: a.reshape(batch, seq, a.shape[-1])
    mixed = pl.pallas_call(
        _mixer_body,
        grid=(seq // ts,),
        in_specs=[tok(DN_WIDTH), tok(DN_WIDTH), tok(DN_WIDTH), tok(DN_WIDTH), tok(D_MODEL), tok(D_MODEL),
                  tok(GATE_PAD), _resident((1, DN_HEAD_DIM))],
        out_specs=tok(D_MODEL),
        out_shape=jax.ShapeDtypeStruct((batch, seq, D_MODEL), BF16),
        scratch_shapes=[pltpu.VMEM((batch * N_PAIRS, DN_HEAD_DIM, PAIR), F32)],
        compiler_params=pltpu.CompilerParams(dimension_semantics=("arbitrary",),
                                             vmem_limit_bytes=VMEM_LIMIT),
        name="mixer",
    )(per_seq(q), per_seq(k), per_seq(v), per_seq(sz), per_seq(sgb), per_seq(yag), per_seq(bg), o_norm_w)
    return mixed.reshape(batch * seq, D_MODEL)


def _ffn_body(alpha, x_ref, mixed_ref, p_ref, lng_ref, lnb_ref, wout_ref, ln1g_ref, ln1b_ref,
              wup_ref, wdown_ref, wg_ref, wp_ref, ln2g_ref, ln2b_ref, o_ref):
    half = x_ref.shape[0] // 2
    n_ff = D_FF // 1024

    def prologue(rows):
        h = _layer_norm(x_ref[rows, :], lng_ref[...], lnb_ref[...])
        t = alpha * h + _dot(mixed_ref[rows, :], wout_ref[...])
        h1 = _layer_norm(t, ln1g_ref[...], ln1b_ref[...])
        return h1.astype(BF16), alpha * h1

    def mlp_chunk(h1b, r, c):
        up = _dot(h1b, wup_ref[:, c * 1024:(c + 1) * 1024])
        act = jnp.square(jnp.maximum(up, 0.0)).astype(BF16)
        return r + _dot(act, wdown_ref[c * 1024:(c + 1) * 1024, :])

    def epilogue(r, rows):
        gate = _sigmoid(_dot(r.astype(BF16), wg_ref[...]))
        ple = gate * _dot(p_ref[rows, :].astype(BF16), wp_ref[...])
        o_ref[rows, :] = _layer_norm(r + ple, ln2g_ref[...], ln2b_ref[...])

    rows_a, rows_b = slice(0, half), slice(half, 2 * half)
    h1b_a, r_a = prologue(rows_a)
    r_a = mlp_chunk(h1b_a, r_a, 0)
    h1b_b, r_b = prologue(rows_b)
    for c in range(1, n_ff):
        r_a = mlp_chunk(h1b_a, r_a, c)
    r_b = mlp_chunk(h1b_b, r_b, 0)
    epilogue(r_a, rows_a)
    for c in range(1, n_ff):
        r_b = mlp_chunk(h1b_b, r_b, c)
    epilogue(r_b, rows_b)


def _channel_mixer(alpha, x2, mixed, p2, ln_g, ln_b, w_out, ln1_g, ln1_b, w_up, w_down, w_g, w_p, ln2_g, ln2_b):
    t = x2.shape[0]
    tm = TILE_FFN
    tok = lambda n: pl.BlockSpec((tm, n), lambda i: (i, 0))
    vec = _resident((1, D_MODEL))
    return pl.pallas_call(
        functools.partial(_ffn_body, alpha),
        grid=(t // tm,),
        in_specs=[tok(D_MODEL), tok(D_MODEL), tok(PLE_DIM), vec, vec,
                  _resident((D_MODEL, D_MODEL)), vec, vec,
                  _resident((D_MODEL, D_FF)), _resident((D_FF, D_MODEL)),
                  _resident((D_MODEL, D_MODEL)), _resident((PLE_DIM, D_MODEL)), vec, vec],
        out_specs=tok(D_MODEL),
        out_shape=jax.ShapeDtypeStruct((t, D_MODEL), F32),
        compiler_params=pltpu.CompilerParams(dimension_semantics=("arbitrary",),
                                             vmem_limit_bytes=VMEM_LIMIT),
        name="channel_mixer",
    )(x2, mixed, p2, ln_g, ln_b, w_out, ln1_g, ln1_b, w_up, w_down, w_g, w_p, ln2_g, ln2_b)


def kernel(x, p, ln_in_g, ln_in_b, w_in, pool_w, pool_scale, conv_w, a_log, dt_bias, o_norm_w, w_out,
           ln1_g, ln1_b, w_up, w_down, ple_gate_w, ple_proj_w, ln2_g, ln2_b):
    batch, seq, _ = x.shape
    depth = w_in.shape[0]
    assert depth == 1, "the fused input layer norm assumes a single layer"
    alpha = (2.0 * depth) ** 0.25
    tokens = batch * seq
    x2 = x.reshape(tokens, D_MODEL)
    row = lambda v: v.reshape(1, -1).astype(F32)

    i = 0
    o_beta = POOL_WIDTH + QKV_WIDTH + DN_WIDTH
    o_ga = o_beta + 2 * DN_HEADS
    w = w_in[i]
    w_main = w[:, :o_beta].astype(BF16)
    w_gates = jnp.concatenate(
        [w[:, o_ga:], w[:, o_beta:o_ga],
         jnp.zeros((D_MODEL, GATE_PAD - 2 * DN_HEADS), w.dtype)], axis=1).astype(BF16)
    pad = jnp.zeros((GATE_PAD - 2 * DN_HEADS,), F32)
    zeros_h = jnp.zeros((DN_HEADS,), F32)
    alog_vec = jnp.concatenate([zeros_h, a_log[i].astype(F32), pad]).reshape(1, GATE_PAD)
    dtb_vec = jnp.concatenate([zeros_h, dt_bias[i].astype(F32), pad]).reshape(1, GATE_PAD)
    yag, q, k, v, sz, sgb, bg = _in_projection(
        x2, row(ln_in_g), row(ln_in_b), w_main, w_gates, pool_w[i].astype(BF16), row(pool_scale[i]),
        conv_w[i].astype(F32), alog_vec, dtb_vec, seq)

    mixed = _mixer(q, k, v, sz, sgb, yag, bg, row(o_norm_w[i]), batch, seq)

    out = _channel_mixer(alpha, x2, mixed, p[i].reshape(tokens, PLE_DIM), row(ln_in_g), row(ln_in_b),
                         w_out[i].astype(BF16), row(ln1_g[i]), row(ln1_b[i]),
                         w_up[i].astype(BF16), w_down[i].astype(BF16),
                         ple_gate_w[i].astype(BF16), ple_proj_w[i].astype(BF16),
                         row(ln2_g[i]), row(ln2_b[i]))
    return out.reshape(batch, seq, D_MODEL)
```

```python
import functools

import jax
import jax.numpy as jnp
from jax import lax
from jax.experimental import pallas as pl
from jax.experimental.pallas import tpu as pltpu

F32 = jnp.float32
BF16 = jnp.bfloat16

D_MODEL = 1024
POOL_WINDOWS = (2, 4, 8, 16)
N_POOL_GROUPS = 4
POOL_WIDTH = D_MODEL // 2
POOL_GROUP = POOL_WIDTH // N_POOL_GROUPS
POOL_OUT_GROUP = D_MODEL // N_POOL_GROUPS
DN_HEADS = 8
DN_HEAD_DIM = 128
DN_WIDTH = DN_HEADS * DN_HEAD_DIM
CONV_K = 4
D_FF = 4 * D_MODEL
PLE_DIM = 256
LN_EPS = 1e-5
RMS_EPS = 1e-6
L2_EPS = 1e-6
QKV_WIDTH = 3 * DN_WIDTH
LOG2E = 1.4426950408889634

LANES = 128
PAIR = 2 * LANES
N_PAIRS = DN_HEADS // 2
GATE_PAD = LANES
POOL_HIST = 16
CONV_HIST = 8

CHUNK = 128
TILE_PROJ = 256
TILE_MIX = 256
TILE_FFN = 1024
VMEM_LIMIT = 56 * 1024 * 1024

_O_QKV = POOL_WIDTH
_O_Z = _O_QKV + QKV_WIDTH
_O_GA = _O_Z + DN_WIDTH
_O_GB = _O_GA + D_MODEL
_O_BG = _O_GB + D_MODEL
_N_PROJ = _O_BG + GATE_PAD


def _layer_norm(x, g, b):
    mu = jnp.mean(x, axis=-1, keepdims=True)
    xc = x - mu
    var = jnp.mean(xc * xc, axis=-1, keepdims=True)
    return xc * lax.rsqrt(var + LN_EPS) * g + b


def _sigmoid(x):
    return 1.0 / (1.0 + jnp.exp2(x * (-LOG2E)))


def _softplus(x):
    return jnp.maximum(x, 0.0) + jnp.log1p(jnp.exp(-jnp.abs(x)))


def _dot(a, b):
    return jnp.dot(a, b, preferred_element_type=F32)


def _resident(shape):
    return pl.BlockSpec(shape, lambda *_: (0,) * len(shape), pipeline_mode=pl.Buffered(1))


def _proj_body(tiles_per_seq, x_ref, g_ref, b_ref, w_ref, wg_ref, poolw_ref, pscale_ref, convw_ref, alog_ref,
               dtb_ref, onw_ref, yag_ref, q_ref, k_ref, v_ref, gm_ref, bg_ref,
               ubuf_ref, qbuf_ref):
    tm = x_ref.shape[0]
    s = pl.program_id(0) % tiles_per_seq

    @pl.when(s == 0)
    def _():
        ubuf_ref[0:POOL_HIST, :] = jnp.zeros((POOL_HIST, POOL_WIDTH), F32)
        qbuf_ref[0:CONV_HIST, :] = jnp.zeros((CONV_HIST, QKV_WIDTH), F32)

    h = _layer_norm(x_ref[...], g_ref[...], b_ref[...]).astype(BF16)

    def proj(c0, n):
        if c0 < _O_GA:
            return _dot(h, w_ref[:, c0:c0 + n])
        return _dot(h, wg_ref[:, c0 - _O_GA:c0 - _O_GA + n])

    tpos = s * tm + lax.broadcasted_iota(jnp.int32, (tm, 1), 0)

    def pool_slice(c0):
        ubuf_ref[POOL_HIST:POOL_HIST + tm, c0:c0 + PAIR] = proj(c0, PAIR)

    def pool_group(gi):
        w = POOL_WINDOWS[gi]
        ue = ubuf_ref[0:POOL_HIST + tm, gi * POOL_GROUP:(gi + 1) * POOL_GROUP]
        win, span = ue, 1
        while span < w:
            win = win + pltpu.roll(win, span, axis=0)
            span *= 2
        cnt = jnp.minimum(tpos + 1, w).astype(F32)
        d = win[POOL_HIST:POOL_HIST + tm] / cnt - ue[POOL_HIST:POOL_HIST + tm]
        ocols = slice(gi * POOL_OUT_GROUP, (gi + 1) * POOL_OUT_GROUP)
        ya = _dot(d.astype(BF16), poolw_ref[gi]) * pscale_ref[:, ocols]
        ga = proj(_O_GA + gi * POOL_OUT_GROUP, POOL_OUT_GROUP)
        yag_ref[:, ocols] = _sigmoid(ga) * ya

    def gate_slice(c0, is_z):
        if is_z:
            pre = proj(_O_Z + c0, PAIR)
            gm_ref[:, c0:c0 + PAIR] = pre * _sigmoid(pre) * onw_ref[:, c0:c0 + PAIR]
        else:
            gm_ref[:, c0:c0 + PAIR] = gm_ref[:, c0:c0 + PAIR] * _sigmoid(proj(_O_GB + c0, PAIR))

    for c0 in range(0, POOL_WIDTH, PAIR):
        pool_slice(c0)
    for gi in range(N_POOL_GROUPS):
        pool_group(gi)

    side = ([functools.partial(gate_slice, c0, True) for c0 in range(0, DN_WIDTH, PAIR)]
            + [functools.partial(gate_slice, c0, False) for c0 in range(0, D_MODEL, PAIR)])
    n_groups = QKV_WIDTH // LANES
    n_slices = QKV_WIDTH // PAIR

    def issue_slice(sl):
        qbuf_ref[CONV_HIST:CONV_HIST + tm, sl * PAIR:(sl + 1) * PAIR] = proj(_O_QKV + sl * PAIR, PAIR)
        for unit in side[sl * len(side) // n_slices:(sl + 1) * len(side) // n_slices]:
            unit()

    for grp in range(n_groups):
        if grp % 2 == 0:
            issue_slice(grp // 2)
        cols = slice(grp * LANES, (grp + 1) * LANES)
        xe = qbuf_ref[0:CONV_HIST + tm, cols]
        acc = None
        for k in range(CONV_K):
            back = CONV_K - 1 - k
            xs = xe if back == 0 else pltpu.roll(xe, back, axis=0)
            term = xs[CONV_HIST:CONV_HIST + tm] * convw_ref[k:k + 1, cols]
            acc = term if acc is None else acc + term
        y = acc * _sigmoid(acc)
        hcols = slice((grp % DN_HEADS) * LANES, (grp % DN_HEADS + 1) * LANES)
        if grp < DN_HEADS:
            q_ref[:, hcols] = y * (lax.rsqrt(jnp.sum(y * y, axis=-1, keepdims=True) + L2_EPS)
                                   * (DN_HEAD_DIM ** -0.5))
        elif grp < 2 * DN_HEADS:
            k_ref[:, hcols] = y * lax.rsqrt(jnp.sum(y * y, axis=-1, keepdims=True) + L2_EPS)
        else:
            v_ref[:, hcols] = y
    qbuf_ref[0:CONV_HIST, :] = qbuf_ref[tm:tm + CONV_HIST, :]
    ubuf_ref[0:POOL_HIST, :] = ubuf_ref[tm:tm + POOL_HIST, :]

    ba = proj(_O_BG, GATE_PAD)
    lane = lax.broadcasted_iota(jnp.int32, (1, GATE_PAD), 1)
    is_decay = (lane >= DN_HEADS) & (lane < 2 * DN_HEADS)
    a_scale = jnp.where(is_decay, jnp.exp(alog_ref[...]), 0.0)
    g = -a_scale * _softplus(ba + dtb_ref[...])
    bg_ref[...] = jnp.where(lane < DN_HEADS, _sigmoid(ba), g)


def _in_projection(x2, ln_g, ln_b, w_main, w_gates, pool_w, pool_scale, conv_w, alog_vec, dtb_vec, onw_row, seq):
    t = x2.shape[0]
    tm = TILE_PROJ
    tok = lambda n: pl.BlockSpec((tm, n), lambda i: (i, 0))
    widths = (D_MODEL, DN_WIDTH, DN_WIDTH, DN_WIDTH, DN_WIDTH, GATE_PAD)
    return pl.pallas_call(
        functools.partial(_proj_body, seq // tm),
        grid=(t // tm,),
        in_specs=[tok(D_MODEL), _resident((1, D_MODEL)), _resident((1, D_MODEL)),
                  _resident(w_main.shape), _resident((D_MODEL, _N_PROJ - _O_GA)),
                  _resident((N_POOL_GROUPS, POOL_GROUP, POOL_OUT_GROUP)), _resident((1, D_MODEL)),
                  _resident((CONV_K, QKV_WIDTH)), _resident((1, GATE_PAD)), _resident((1, GATE_PAD)),
                  _resident((1, DN_WIDTH))],
        out_specs=[tok(n) for n in widths],
        out_shape=[jax.ShapeDtypeStruct((t, n), F32) for n in widths],
        scratch_shapes=[pltpu.VMEM((POOL_HIST + tm, POOL_WIDTH), F32),
                        pltpu.VMEM((CONV_HIST + tm, QKV_WIDTH), F32)],
        compiler_params=pltpu.CompilerParams(dimension_semantics=("arbitrary",),
                                             vmem_limit_bytes=VMEM_LIMIT),
        name="in_projection",
    )(x2, ln_g, ln_b, w_main, w_gates, pool_w, pool_scale, conv_w, alog_vec, dtb_vec, onw_row)


def _block_diag(a2):
    z = jnp.zeros((a2.shape[0], LANES), a2.dtype)
    return jnp.concatenate([jnp.concatenate([a2[:, :LANES], z], axis=1),
                            jnp.concatenate([z, a2[:, LANES:]], axis=1)], axis=0)


def _pair_dot(a2, b2):
    return _dot(a2, _block_diag(b2))


def _inverse_masks():
    r = lax.broadcasted_iota(jnp.int32, (CHUNK, PAIR), 0)
    c = lax.broadcasted_iota(jnp.int32, (CHUNK, PAIR), 1) % LANES
    strict = r > c
    eye = jnp.where(r == c, 1.0, 0.0).astype(F32)
    base = strict & ((r // 8) == (c // 8))
    levels = []
    b = 8
    while b < CHUNK:
        levels.append(strict & ((r // (2 * b)) == (c // (2 * b))) & ((r // b) != (c // b)))
        b *= 2
    return eye, base, levels


def _unit_lower_inverse(ms, masks):
    eye, base, levels = masks
    n = [jnp.where(base, m, 0.0) for m in ms]
    p = [eye - x for x in n]
    nb = [x.astype(BF16) for x in n]
    qb = [_pair_dot(x, x).astype(BF16) for x in nb]
    yield
    p = [x + _pair_dot(x.astype(BF16), y) for x, y in zip(p, qb)]
    qb = [_pair_dot(x, x).astype(BF16) for x in qb]
    yield
    t = [x + _pair_dot(x.astype(BF16), y) for x, y in zip(p, qb)]
    yield
    for lm in levels:
        cb = [jnp.where(lm, m, 0.0).astype(BF16) for m in ms]
        tb = [x.astype(BF16) for x in t]
        yb = [_pair_dot(c, x).astype(BF16) for c, x in zip(cb, tb)]
        yield
        t = [x - _pair_dot(xb, y) for x, xb, y in zip(t, tb, yb)]
        yield
    return t


def _interleave(gens):
    gens = list(gens)
    while gens:
        for g in list(gens):
            try:
                next(g)
            except StopIteration:
                gens.remove(g)


def _mixer_body(q_ref, k_ref, v_ref, gm_ref, yag_ref, bg_ref, out_ref, state_ref):
    n_seq, ts = out_ref.shape[0], out_ref.shape[1]
    n_chunks = ts // CHUNK
    units = [(b, ch) for b in range(n_seq) for ch in range(n_chunks)]

    @pl.when(pl.program_id(0) == 0)
    def _():
        state_ref[...] = jnp.zeros_like(state_ref)

    bg = [bg_ref[b] for b in range(n_seq)]
    b_hi = [x.astype(BF16) for x in bg]
    r1 = [x - y.astype(F32) for x, y in zip(bg, b_hi)]
    b_mid = [x.astype(BF16) for x in r1]
    b_lo = [(x - y.astype(F32)).astype(BF16) for x, y in zip(r1, b_mid)]
    ri = lax.broadcasted_iota(jnp.int32, (CHUNK, CHUNK), 0)
    ci = lax.broadcasted_iota(jnp.int32, (CHUNK, CHUNK), 1)
    tri = jnp.where(ri >= ci, 1.0, 0.0).astype(BF16)
    by_unit = lambda a: jnp.concatenate([a[b][ch * CHUNK:(ch + 1) * CHUNK] for b, ch in units], axis=1)
    gc_wide = (_dot(tri, by_unit(b_hi)) + _dot(tri, by_unit(b_mid))
               + _dot(tri, by_unit(b_lo)))
    gc_of = {u: gc_wide[:, i * LANES:(i + 1) * LANES] for i, u in enumerate(units)}
    gct_of = {u: g.T for u, g in gc_of.items()}

    r = lax.broadcasted_iota(jnp.int32, (CHUNK, PAIR), 0)
    c = lax.broadcasted_iota(jnp.int32, (CHUNK, PAIR), 1) % LANES
    incl = r >= c
    strict = r > c
    masks = _inverse_masks()

    def head_rms_scale(x2):
        parts = []
        for j in range(2):
            xh = x2[:, j * LANES:(j + 1) * LANES]
            ms = jnp.sum(xh * xh, axis=-1, keepdims=True) * (1.0 / DN_HEAD_DIM)
            parts.append(jnp.broadcast_to(lax.rsqrt(ms + RMS_EPS), (CHUNK, LANES)))
        return jnp.concatenate(parts, axis=1)

    def lanes_of(arr, rows, p, lane0):
        return jnp.concatenate(
            [jnp.broadcast_to(arr[rows, lane0 + 2 * p + j:lane0 + 2 * p + j + 1], (CHUNK, LANES))
             for j in range(2)], axis=1)

    rows_of = lambda ch: slice(ch * CHUNK, (ch + 1) * CHUNK)
    cols_of = lambda p: slice(p * PAIR, (p + 1) * PAIR)
    ready = {}

    def prepare():
        probs = [(b, ch, p) for b, ch in units for p in range(N_PAIRS)]
        k2 = [k_ref[b, rows_of(ch), cols_of(p)] for b, ch, p in probs]
        beta2 = [lanes_of(bg[b], rows_of(ch), p, 0) for b, ch, p in probs]
        gc2 = [lanes_of(gc_of[b, ch], slice(0, CHUNK), p, DN_HEADS) for b, ch, p in probs]
        gl2 = [lanes_of(gc_of[b, ch], slice(CHUNK - 1, CHUNK), p, DN_HEADS) for b, ch, p in probs]
        gcrow2 = [jnp.concatenate(
            [jnp.broadcast_to(gct_of[b, ch][DN_HEADS + 2 * p + j:DN_HEADS + 2 * p + j + 1, :],
                              (CHUNK, CHUNK)) for j in range(2)], axis=1) for b, ch, p in probs]
        decay2 = [jnp.where(incl, jnp.exp(a - b), 0.0) for a, b in zip(gc2, gcrow2)]
        egc2 = [jnp.exp(x) for x in gc2]
        kb2 = [x * y for x, y in zip(k2, beta2)]
        kbf = [x.astype(BF16) for x in k2]
        qbf = [q_ref[b, rows_of(ch), cols_of(p)].astype(BF16) for b, ch, p in probs]
        aq = [lax.dot_general(jnp.concatenate([x.astype(BF16), y], axis=0), _block_diag(z),
                              (((1,), (1,)), ((), ())), preferred_element_type=F32)
              for x, y, z in zip(kb2, qbf, kbf)]
        yield
        m2 = [jnp.where(strict, x[0:CHUNK] * d, 0.0) for x, d in zip(aq, decay2)]
        attn_b = [(x[CHUNK:2 * CHUNK] * d).astype(BF16) for x, d in zip(aq, decay2)]
        t2 = yield from _unit_lower_inverse(m2, masks)
        t_b = [x.astype(BF16) for x in t2]
        u2 = [_pair_dot(t, (v_ref[b, rows_of(ch), cols_of(p)] * bt).astype(BF16))
              for t, bt, (b, ch, p) in zip(t_b, beta2, probs)]
        w2 = [_pair_dot(t, (x * e).astype(BF16)) for t, x, e in zip(t_b, kb2, egc2)]
        yield
        wqg_b = [jnp.concatenate([w.astype(BF16), (q_ref[b, rows_of(ch), cols_of(p)] * e).astype(BF16)], axis=0)
                 for w, e, (b, ch, p) in zip(w2, egc2, probs)]
        kgt_b = [(x * jnp.exp(gl - gc)).T.astype(BF16)
                 for x, gl, gc in zip(k2, gl2, gc2)]
        egl2 = [jnp.exp(x) for x in gl2]
        for i, key in enumerate(probs):
            ready[key] = (u2[i], wqg_b[i], attn_b[i], kgt_b[i], egl2[i])
        yield

    lanes = [(b, p) for b in range(n_seq) for p in range(N_PAIRS)]
    states = [state_ref[b * N_PAIRS + p] for b, p in lanes]

    def recur():
        for ch in range(n_chunks):
            rows = rows_of(ch)
            u2, wqg_b, attn_b, kgt_b, egl2 = zip(*[ready[(b, ch, p)] for b, p in lanes])
            wq = [_pair_dot(a, st.astype(BF16)) for a, st in zip(wqg_b, states)]
            yield
            v_new_b = [(u - x[0:CHUNK]).astype(BF16) for u, x in zip(u2, wq)]
            o2 = [x[CHUNK:2 * CHUNK] + _pair_dot(a, vn) for x, a, vn in zip(wq, attn_b, v_new_b)]
            full = [_dot(kt, vn) for kt, vn in zip(kgt_b, v_new_b)]
            yield
            for i, f in enumerate(full):
                states[i] = states[i] * egl2[i] + jnp.concatenate(
                    [f[0:LANES, 0:LANES], f[LANES:PAIR, LANES:PAIR]], axis=1)
            yield
            for i, (b, p) in enumerate(lanes):
                cols = cols_of(p)
                yb = o2[i] * head_rms_scale(o2[i]) * gm_ref[b, rows, cols]
                out_ref[b, rows, cols] = (yag_ref[b, rows, cols] + yb).astype(out_ref.dtype)

    _interleave([prepare()])
    _interleave([recur()])

    for i, (b, p) in enumerate(lanes):
        state_ref[b * N_PAIRS + p] = states[i]


def _mixer(q, k, v, gm, yag, bg, batch, seq):
    ts = TILE_MIX
    tok = lambda n: pl.BlockSpec((batch, ts, n), lambda s: (0, s, 0))
    per_seq = lambda a: a.reshape(batch, seq, a.shape[-1])
    mixed = pl.pallas_call(
        _mixer_body,
        grid=(seq // ts,),
        in_specs=[tok(DN_WIDTH), tok(DN_WIDTH), tok(DN_WIDTH), tok(DN_WIDTH), tok(D_MODEL), tok(GATE_PAD)],
        out_specs=tok(D_MODEL),
        out_shape=jax.ShapeDtypeStruct((batch, seq, D_MODEL), BF16),
        scratch_shapes=[pltpu.VMEM((batch * N_PAIRS, DN_HEAD_DIM, PAIR), F32)],
        compiler_params=pltpu.CompilerParams(dimension_semantics=("arbitrary",),
                                             vmem_limit_bytes=VMEM_LIMIT),
        name="mixer",
    )(per_seq(q), per_seq(k), per_seq(v), per_seq(gm), per_seq(yag), per_seq(bg))
    return mixed.reshape(batch * seq, D_MODEL)


def _ffn_body(alpha, x_ref, mixed_ref, p_ref, lng_ref, lnb_ref, wout_ref, ln1g_ref, ln1b_ref,
              wup_ref, wdown_ref, wg_ref, wp_ref, ln2g_ref, ln2b_ref, o_ref):
    half = x_ref.shape[0] // 2
    n_ff = D_FF // 1024

    def prologue(rows):
        h = _layer_norm(x_ref[rows, :], lng_ref[...], lnb_ref[...])
        t = alpha * h + _dot(mixed_ref[rows, :], wout_ref[...])
        h1 = _layer_norm(t, ln1g_ref[...], ln1b_ref[...])
        return h1.astype(BF16), alpha * h1

    def mlp_chunk(h1b, r, c):
        up = _dot(h1b, wup_ref[:, c * 1024:(c + 1) * 1024])
        act = jnp.square(jnp.maximum(up, 0.0)).astype(BF16)
        return r + _dot(act, wdown_ref[c * 1024:(c + 1) * 1024, :])

    def epilogue(r, rows):
        gate = _sigmoid(_dot(r.astype(BF16), wg_ref[...]))
        ple = gate * _dot(p_ref[rows, :].astype(BF16), wp_ref[...])
        o_ref[rows, :] = _layer_norm(r + ple, ln2g_ref[...], ln2b_ref[...])

    rows_a, rows_b = slice(0, half), slice(half, 2 * half)
    h1b_a, r_a = prologue(rows_a)
    h1b_b, r_b = prologue(rows_b)
    for c in range(n_ff):
        r_a = mlp_chunk(h1b_a, r_a, c)
        r_b = mlp_chunk(h1b_b, r_b, c)
    epilogue(r_a, rows_a)
    epilogue(r_b, rows_b)


def _channel_mixer(alpha, x2, mixed, p2, ln_g, ln_b, w_out, ln1_g, ln1_b, w_up, w_down, w_g, w_p, ln2_g, ln2_b):
    t = x2.shape[0]
    tm = TILE_FFN
    tok = lambda n: pl.BlockSpec((tm, n), lambda i: (i, 0))
    vec = _resident((1, D_MODEL))
    return pl.pallas_call(
        functools.partial(_ffn_body, alpha),
        grid=(t // tm,),
        in_specs=[tok(D_MODEL), tok(D_MODEL), tok(PLE_DIM), vec, vec,
                  _resident((D_MODEL, D_MODEL)), vec, vec,
                  _resident((D_MODEL, D_FF)), _resident((D_FF, D_MODEL)),
                  _resident((D_MODEL, D_MODEL)), _resident((PLE_DIM, D_MODEL)), vec, vec],
        out_specs=tok(D_MODEL),
        out_shape=jax.ShapeDtypeStruct((t, D_MODEL), F32),
        compiler_params=pltpu.CompilerParams(dimension_semantics=("arbitrary",),
                                             vmem_limit_bytes=VMEM_LIMIT),
        name="channel_mixer",
    )(x2, mixed, p2, ln_g, ln_b, w_out, ln1_g, ln1_b, w_up, w_down, w_g, w_p, ln2_g, ln2_b)


def kernel(x, p, ln_in_g, ln_in_b, w_in, pool_w, pool_scale, conv_w, a_log, dt_bias, o_norm_w, w_out,
           ln1_g, ln1_b, w_up, w_down, ple_gate_w, ple_proj_w, ln2_g, ln2_b):
    batch, seq, _ = x.shape
    depth = w_in.shape[0]
    assert depth == 1, "the fused input layer norm assumes a single layer"
    alpha = (2.0 * depth) ** 0.25
    tokens = batch * seq
    x2 = x.reshape(tokens, D_MODEL)
    row = lambda v: v.reshape(1, -1).astype(F32)

    i = 0
    o_beta = POOL_WIDTH + QKV_WIDTH + DN_WIDTH
    o_ga = o_beta + 2 * DN_HEADS
    w = w_in[i]
    w_main = w.astype(BF16)
    w_gates = jnp.concatenate(
        [w[:, o_ga:], w[:, o_beta:o_ga],
         jnp.zeros((D_MODEL, GATE_PAD - 2 * DN_HEADS), w.dtype)], axis=1).astype(BF16)
    pad = jnp.zeros((GATE_PAD - 2 * DN_HEADS,), F32)
    zeros_h = jnp.zeros((DN_HEADS,), F32)
    alog_vec = jnp.concatenate([zeros_h, a_log[i].astype(F32), pad]).reshape(1, GATE_PAD)
    dtb_vec = jnp.concatenate([zeros_h, dt_bias[i].astype(F32), pad]).reshape(1, GATE_PAD)
    yag, q, k, v, gm, bg = _in_projection(
        x2, row(ln_in_g), row(ln_in_b), w_main, w_gates, pool_w[i].astype(BF16), row(pool_scale[i]),
        conv_w[i].astype(F32), alog_vec, dtb_vec, row(jnp.tile(o_norm_w[i], DN_HEADS)), seq)

    mixed = _mixer(q, k, v, gm, yag, bg, batch, seq)

    out = _channel_mixer(alpha, x2, mixed, p[i].reshape(tokens, PLE_DIM), row(ln_in_g), row(ln_in_b),
                         w_out[i].astype(BF16), row(ln1_g[i]), row(ln1_b[i]),
                         w_up[i].astype(BF16), w_down[i].astype(BF16),
                         ple_gate_w[i].astype(BF16), ple_proj_w[i].astype(BF16),
                         row(ln2_g[i]), row(ln2_b[i]))
    return out.reshape(batch, seq, D_MODEL)
```

```python
import functools

import jax
import jax.numpy as jnp
from jax import lax
from jax.experimental import pallas as pl
from jax.experimental.pallas import tpu as pltpu

F32 = jnp.float32
BF16 = jnp.bfloat16

D_MODEL = 1024
POOL_WINDOWS = (2, 4, 8, 16)
N_POOL_GROUPS = 4
POOL_WIDTH = D_MODEL // 2
POOL_GROUP = POOL_WIDTH // N_POOL_GROUPS
POOL_OUT_GROUP = D_MODEL // N_POOL_GROUPS
DN_HEADS = 8
DN_HEAD_DIM = 128
DN_WIDTH = DN_HEADS * DN_HEAD_DIM
CONV_K = 4
D_FF = 4 * D_MODEL
PLE_DIM = 256
LN_EPS = 1e-5
RMS_EPS = 1e-6
L2_EPS = 1e-6
QKV_WIDTH = 3 * DN_WIDTH
LOG2E = 1.4426950408889634

LANES = 128
PAIR = 2 * LANES
N_PAIRS = DN_HEADS // 2
GATE_PAD = LANES
POOL_HIST = 16
CONV_HIST = 8

CHUNK = 128
TILE_PROJ = 256
TILE_MIX = 256
TILE_FFN = 512
FF_CHUNK = 1024
VMEM_LIMIT = 56 * 1024 * 1024

_O_QKV = POOL_WIDTH
_O_Z = _O_QKV + QKV_WIDTH
_O_GA = _O_Z + DN_WIDTH
_O_GB = _O_GA + D_MODEL
_O_BG = _O_GB + D_MODEL
_N_PROJ = _O_BG + GATE_PAD


def _layer_norm(x, g, b):
    mu = jnp.mean(x, axis=-1, keepdims=True)
    xc = x - mu
    var = jnp.mean(xc * xc, axis=-1, keepdims=True)
    return xc * lax.rsqrt(var + LN_EPS) * g + b


def _sigmoid(x):
    return 1.0 / (1.0 + jnp.exp2(x * (-LOG2E)))


def _softplus(x):
    return jnp.maximum(x, 0.0) + jnp.log1p(jnp.exp(-jnp.abs(x)))


def _dot(a, b):
    return jnp.dot(a, b, preferred_element_type=F32)


def _resident(shape):
    return pl.BlockSpec(shape, lambda *_: (0,) * len(shape), pipeline_mode=pl.Buffered(1))


def _proj_body(tiles_per_seq, x_ref, g_ref, b_ref, w_ref, wg_ref, poolw_ref, pscale_ref, convw_ref, alog_ref,
               dtb_ref, onw_ref, yag_ref, q_ref, k_ref, v_ref, gm_ref, bg_ref,
               ubuf_ref, qbuf_ref):
    tm = x_ref.shape[0]
    s = pl.program_id(0) % tiles_per_seq

    @pl.when(s == 0)
    def _():
        ubuf_ref[0:POOL_HIST, :] = jnp.zeros((POOL_HIST, POOL_WIDTH), F32)
        qbuf_ref[0:CONV_HIST, :] = jnp.zeros((CONV_HIST, QKV_WIDTH), F32)

    h = _layer_norm(x_ref[...], g_ref[...], b_ref[...]).astype(BF16)

    def proj(c0, n):
        if c0 < _O_GA:
            return _dot(h, w_ref[:, c0:c0 + n])
        return _dot(h, wg_ref[:, c0 - _O_GA:c0 - _O_GA + n])

    tpos = s * tm + lax.broadcasted_iota(jnp.int32, (tm, 1), 0)

    def pool_slice(c0):
        ubuf_ref[POOL_HIST:POOL_HIST + tm, c0:c0 + PAIR] = proj(c0, PAIR)

    def pool_group(gi):
        w = POOL_WINDOWS[gi]
        ue = ubuf_ref[0:POOL_HIST + tm, gi * POOL_GROUP:(gi + 1) * POOL_GROUP]
        win, span = ue, 1
        while span < w:
            win = win + pltpu.roll(win, span, axis=0)
            span *= 2
        cnt = jnp.minimum(tpos + 1, w).astype(F32)
        d = win[POOL_HIST:POOL_HIST + tm] / cnt - ue[POOL_HIST:POOL_HIST + tm]
        ocols = slice(gi * POOL_OUT_GROUP, (gi + 1) * POOL_OUT_GROUP)
        ya = _dot(d.astype(BF16), poolw_ref[gi]) * pscale_ref[:, ocols]
        ga = proj(_O_GA + gi * POOL_OUT_GROUP, POOL_OUT_GROUP)
        yag_ref[:, ocols] = _sigmoid(ga) * ya

    def gate_slice(c0, is_z):
        if is_z:
            pre = proj(_O_Z + c0, PAIR)
            gm_ref[:, c0:c0 + PAIR] = pre * _sigmoid(pre) * onw_ref[:, c0:c0 + PAIR]
        else:
            gm_ref[:, c0:c0 + PAIR] = gm_ref[:, c0:c0 + PAIR] * _sigmoid(proj(_O_GB + c0, PAIR))

    for c0 in range(0, POOL_WIDTH, PAIR):
        pool_slice(c0)
    for gi in range(N_POOL_GROUPS):
        pool_group(gi)

    side = ([functools.partial(gate_slice, c0, True) for c0 in range(0, DN_WIDTH, PAIR)]
            + [functools.partial(gate_slice, c0, False) for c0 in range(0, D_MODEL, PAIR)])
    n_groups = QKV_WIDTH // LANES
    n_slices = QKV_WIDTH // PAIR

    def issue_slice(sl):
        qbuf_ref[CONV_HIST:CONV_HIST + tm, sl * PAIR:(sl + 1) * PAIR] = proj(_O_QKV + sl * PAIR, PAIR)
        for unit in side[sl * len(side) // n_slices:(sl + 1) * len(side) // n_slices]:
            unit()

    for grp in range(n_groups):
        if grp % 2 == 0:
            issue_slice(grp // 2)
        cols = slice(grp * LANES, (grp + 1) * LANES)
        xe = qbuf_ref[0:CONV_HIST + tm, cols]
        acc = None
        for k in range(CONV_K):
            back = CONV_K - 1 - k
            xs = xe if back == 0 else pltpu.roll(xe, back, axis=0)
            term = xs[CONV_HIST:CONV_HIST + tm] * convw_ref[k:k + 1, cols]
            acc = term if acc is None else acc + term
        y = acc * _sigmoid(acc)
        hcols = slice((grp % DN_HEADS) * LANES, (grp % DN_HEADS + 1) * LANES)
        if grp < DN_HEADS:
            q_ref[:, hcols] = y * (lax.rsqrt(jnp.sum(y * y, axis=-1, keepdims=True) + L2_EPS)
                                   * (DN_HEAD_DIM ** -0.5))
        elif grp < 2 * DN_HEADS:
            k_ref[:, hcols] = y * lax.rsqrt(jnp.sum(y * y, axis=-1, keepdims=True) + L2_EPS)
        else:
            v_ref[:, hcols] = y
    qbuf_ref[0:CONV_HIST, :] = qbuf_ref[tm:tm + CONV_HIST, :]
    ubuf_ref[0:POOL_HIST, :] = ubuf_ref[tm:tm + POOL_HIST, :]

    ba = proj(_O_BG, GATE_PAD)
    lane = lax.broadcasted_iota(jnp.int32, (1, GATE_PAD), 1)
    is_decay = (lane >= DN_HEADS) & (lane < 2 * DN_HEADS)
    a_scale = jnp.where(is_decay, jnp.exp(alog_ref[...]), 0.0)
    g = -a_scale * _softplus(ba + dtb_ref[...])
    bg_ref[...] = jnp.where(lane < DN_HEADS, _sigmoid(ba), g)


def _in_projection(x2, ln_g, ln_b, w_main, w_gates, pool_w, pool_scale, conv_w, alog_vec, dtb_vec, onw_row, seq):
    t = x2.shape[0]
    tm = TILE_PROJ
    tok = lambda n: pl.BlockSpec((tm, n), lambda i: (i, 0))
    widths = (D_MODEL, DN_WIDTH, DN_WIDTH, DN_WIDTH, DN_WIDTH, GATE_PAD)
    return pl.pallas_call(
        functools.partial(_proj_body, seq // tm),
        grid=(t // tm,),
        in_specs=[tok(D_MODEL), _resident((1, D_MODEL)), _resident((1, D_MODEL)),
                  _resident(w_main.shape), _resident((D_MODEL, _N_PROJ - _O_GA)),
                  _resident((N_POOL_GROUPS, POOL_GROUP, POOL_OUT_GROUP)), _resident((1, D_MODEL)),
                  _resident((CONV_K, QKV_WIDTH)), _resident((1, GATE_PAD)), _resident((1, GATE_PAD)),
                  _resident((1, DN_WIDTH))],
        out_specs=[tok(n) for n in widths],
        out_shape=[jax.ShapeDtypeStruct((t, n), F32) for n in widths],
        scratch_shapes=[pltpu.VMEM((POOL_HIST + tm, POOL_WIDTH), F32),
                        pltpu.VMEM((CONV_HIST + tm, QKV_WIDTH), F32)],
        compiler_params=pltpu.CompilerParams(dimension_semantics=("arbitrary",),
                                             vmem_limit_bytes=VMEM_LIMIT),
        name="in_projection",
    )(x2, ln_g, ln_b, w_main, w_gates, pool_w, pool_scale, conv_w, alog_vec, dtb_vec, onw_row)


def _block_diag(a2):
    z = jnp.zeros((a2.shape[0], LANES), a2.dtype)
    return jnp.concatenate([jnp.concatenate([a2[:, :LANES], z], axis=1),
                            jnp.concatenate([z, a2[:, LANES:]], axis=1)], axis=0)


def _pair_dot(a2, b2):
    return _dot(a2, _block_diag(b2))


def _inverse_masks():
    r = lax.broadcasted_iota(jnp.int32, (CHUNK, PAIR), 0)
    c = lax.broadcasted_iota(jnp.int32, (CHUNK, PAIR), 1) % LANES
    strict = r > c
    eye = jnp.where(r == c, 1.0, 0.0).astype(F32)
    base = strict & ((r // 8) == (c // 8))
    levels = []
    b = 8
    while b < CHUNK:
        levels.append(strict & ((r // (2 * b)) == (c // (2 * b))) & ((r // b) != (c // b)))
        b *= 2
    return eye, base, levels


def _unit_lower_inverse(ms, masks):
    eye, base, levels = masks
    n = [jnp.where(base, m, 0.0) for m in ms]
    p = [eye - x for x in n]
    nb = [x.astype(BF16) for x in n]
    qb = [_pair_dot(x, x).astype(BF16) for x in nb]
    yield
    p = [x + _pair_dot(x.astype(BF16), y) for x, y in zip(p, qb)]
    qb = [_pair_dot(x, x).astype(BF16) for x in qb]
    yield
    t = [x + _pair_dot(x.astype(BF16), y) for x, y in zip(p, qb)]
    yield
    for lm in levels:
        cb = [jnp.where(lm, m, 0.0).astype(BF16) for m in ms]
        tb = [x.astype(BF16) for x in t]
        yb = [_pair_dot(c, x).astype(BF16) for c, x in zip(cb, tb)]
        yield
        t = [x - _pair_dot(xb, y) for x, xb, y in zip(t, tb, yb)]
        yield
    return t


def _interleave(gens):
    gens = list(gens)
    while gens:
        for g in list(gens):
            try:
                next(g)
            except StopIteration:
                gens.remove(g)


def _mixer_body(q_ref, k_ref, v_ref, gm_ref, yag_ref, bg_ref, out_ref, state_ref):
    n_seq, ts = out_ref.shape[0], out_ref.shape[1]
    n_chunks = ts // CHUNK
    units = [(b, ch) for b in range(n_seq) for ch in range(n_chunks)]

    @pl.when(pl.program_id(0) == 0)
    def _():
        state_ref[...] = jnp.zeros_like(state_ref)

    bg = [bg_ref[b] for b in range(n_seq)]
    b_hi = [x.astype(BF16) for x in bg]
    r1 = [x - y.astype(F32) for x, y in zip(bg, b_hi)]
    b_mid = [x.astype(BF16) for x in r1]
    b_lo = [(x - y.astype(F32)).astype(BF16) for x, y in zip(r1, b_mid)]
    ri = lax.broadcasted_iota(jnp.int32, (CHUNK, CHUNK), 0)
    ci = lax.broadcasted_iota(jnp.int32, (CHUNK, CHUNK), 1)
    tri = jnp.where(ri >= ci, 1.0, 0.0).astype(BF16)
    by_unit = lambda a: jnp.concatenate([a[b][ch * CHUNK:(ch + 1) * CHUNK] for b, ch in units], axis=1)
    gc_wide = (_dot(tri, by_unit(b_hi)) + _dot(tri, by_unit(b_mid))
               + _dot(tri, by_unit(b_lo)))
    gc_of = {u: gc_wide[:, i * LANES:(i + 1) * LANES] for i, u in enumerate(units)}
    gct_of = {u: g.T for u, g in gc_of.items()}

    r = lax.broadcasted_iota(jnp.int32, (CHUNK, PAIR), 0)
    c = lax.broadcasted_iota(jnp.int32, (CHUNK, PAIR), 1) % LANES
    incl = r >= c
    strict = r > c
    masks = _inverse_masks()

    def head_rms_scale(x2):
        parts = []
        for j in range(2):
            xh = x2[:, j * LANES:(j + 1) * LANES]
            ms = jnp.sum(xh * xh, axis=-1, keepdims=True) * (1.0 / DN_HEAD_DIM)
            parts.append(jnp.broadcast_to(lax.rsqrt(ms + RMS_EPS), (CHUNK, LANES)))
        return jnp.concatenate(parts, axis=1)

    def lanes_of(arr, rows, p, lane0):
        return jnp.concatenate(
            [jnp.broadcast_to(arr[rows, lane0 + 2 * p + j:lane0 + 2 * p + j + 1], (CHUNK, LANES))
             for j in range(2)], axis=1)

    rows_of = lambda ch: slice(ch * CHUNK, (ch + 1) * CHUNK)
    cols_of = lambda p: slice(p * PAIR, (p + 1) * PAIR)
    ready = {}

    def prepare():
        probs = [(b, ch, p) for b, ch in units for p in range(N_PAIRS)]
        k2 = [k_ref[b, rows_of(ch), cols_of(p)] for b, ch, p in probs]
        beta2 = [lanes_of(bg[b], rows_of(ch), p, 0) for b, ch, p in probs]
        gc2 = [lanes_of(gc_of[b, ch], slice(0, CHUNK), p, DN_HEADS) for b, ch, p in probs]
        gl2 = [lanes_of(gc_of[b, ch], slice(CHUNK - 1, CHUNK), p, DN_HEADS) for b, ch, p in probs]
        gcrow2 = [jnp.concatenate(
            [jnp.broadcast_to(gct_of[b, ch][DN_HEADS + 2 * p + j:DN_HEADS + 2 * p + j + 1, :],
                              (CHUNK, CHUNK)) for j in range(2)], axis=1) for b, ch, p in probs]
        decay2 = [jnp.where(incl, jnp.exp(a - b), 0.0) for a, b in zip(gc2, gcrow2)]
        egc2 = [jnp.exp(x) for x in gc2]
        kb2 = [x * y for x, y in zip(k2, beta2)]
        kbf = [x.astype(BF16) for x in k2]
        qbf = [q_ref[b, rows_of(ch), cols_of(p)].astype(BF16) for b, ch, p in probs]
        aq = [lax.dot_general(jnp.concatenate([x.astype(BF16), y], axis=0), _block_diag(z),
                              (((1,), (1,)), ((), ())), preferred_element_type=F32)
              for x, y, z in zip(kb2, qbf, kbf)]
        yield
        m2 = [jnp.where(strict, x[0:CHUNK] * d, 0.0) for x, d in zip(aq, decay2)]
        attn_b = [(x[CHUNK:2 * CHUNK] * d).astype(BF16) for x, d in zip(aq, decay2)]
        t2 = yield from _unit_lower_inverse(m2, masks)
        t_b = [x.astype(BF16) for x in t2]
        u2 = [_pair_dot(t, (v_ref[b, rows_of(ch), cols_of(p)] * bt).astype(BF16))
              for t, bt, (b, ch, p) in zip(t_b, beta2, probs)]
        w2 = [_pair_dot(t, (x * e).astype(BF16)) for t, x, e in zip(t_b, kb2, egc2)]
        yield
        wqg_b = [jnp.concatenate([w.astype(BF16), (q_ref[b, rows_of(ch), cols_of(p)] * e).astype(BF16)], axis=0)
                 for w, e, (b, ch, p) in zip(w2, egc2, probs)]
        kgt_b = [(x * jnp.exp(gl - gc)).T.astype(BF16)
                 for x, gl, gc in zip(k2, gl2, gc2)]
        egl2 = [jnp.exp(x) for x in gl2]
        for i, key in enumerate(probs):
            ready[key] = (u2[i], wqg_b[i], attn_b[i], kgt_b[i], egl2[i])
        yield

    lanes = [(b, p) for b in range(n_seq) for p in range(N_PAIRS)]
    states = [state_ref[b * N_PAIRS + p] for b, p in lanes]

    def recur():
        for ch in range(n_chunks):
            rows = rows_of(ch)
            u2, wqg_b, attn_b, kgt_b, egl2 = zip(*[ready[(b, ch, p)] for b, p in lanes])
            wq = [_pair_dot(a, st.astype(BF16)) for a, st in zip(wqg_b, states)]
            yield
            v_new_b = [(u - x[0:CHUNK]).astype(BF16) for u, x in zip(u2, wq)]
            o2 = [x[CHUNK:2 * CHUNK] + _pair_dot(a, vn) for x, a, vn in zip(wq, attn_b, v_new_b)]
            full = [_dot(kt, vn) for kt, vn in zip(kgt_b, v_new_b)]
            yield
            for i, f in enumerate(full):
                states[i] = states[i] * egl2[i] + jnp.concatenate(
                    [f[0:LANES, 0:LANES], f[LANES:PAIR, LANES:PAIR]], axis=1)
            yield
            for i, (b, p) in enumerate(lanes):
                cols = cols_of(p)
                yb = o2[i] * head_rms_scale(o2[i]) * gm_ref[b, rows, cols]
                out_ref[b, rows, cols] = (yag_ref[b, rows, cols] + yb).astype(out_ref.dtype)

    _interleave([prepare()])
    _interleave([recur()])

    for i, (b, p) in enumerate(lanes):
        state_ref[b * N_PAIRS + p] = states[i]


def _mixer(q, k, v, gm, yag, bg, batch, seq):
    ts = TILE_MIX
    tok = lambda n: pl.BlockSpec((batch, ts, n), lambda s: (0, s, 0))
    per_seq = lambda a: a.reshape(batch, seq, a.shape[-1])
    mixed = pl.pallas_call(
        _mixer_body,
        grid=(seq // ts,),
        in_specs=[tok(DN_WIDTH), tok(DN_WIDTH), tok(DN_WIDTH), tok(DN_WIDTH), tok(D_MODEL), tok(GATE_PAD)],
        out_specs=tok(D_MODEL),
        out_shape=jax.ShapeDtypeStruct((batch, seq, D_MODEL), BF16),
        scratch_shapes=[pltpu.VMEM((batch * N_PAIRS, DN_HEAD_DIM, PAIR), F32)],
        compiler_params=pltpu.CompilerParams(dimension_semantics=("arbitrary",),
                                             vmem_limit_bytes=VMEM_LIMIT),
        name="mixer",
    )(per_seq(q), per_seq(k), per_seq(v), per_seq(gm), per_seq(yag), per_seq(bg))
    return mixed.reshape(batch * seq, D_MODEL)


def _ffn_body(alpha, x_ref, mixed_ref, p_ref, lng_ref, lnb_ref, wout_ref, ln1g_ref, ln1b_ref,
              wup_ref, wdown_ref, wg_ref, wp_ref, ln2g_ref, ln2b_ref, o_ref):
    half = x_ref.shape[0] // 2
    n_ff = D_FF // FF_CHUNK

    def prologue(rows):
        h = _layer_norm(x_ref[rows, :], lng_ref[...], lnb_ref[...])
        t = alpha * h + _dot(mixed_ref[rows, :], wout_ref[...])
        h1 = _layer_norm(t, ln1g_ref[...], ln1b_ref[...])
        return h1.astype(BF16), alpha * h1

    def mlp_chunk(h1b, r, c):
        up = _dot(h1b, wup_ref[:, c * FF_CHUNK:(c + 1) * FF_CHUNK])
        act = jnp.square(jnp.maximum(up, 0.0)).astype(BF16)
        return r + _dot(act, wdown_ref[c * FF_CHUNK:(c + 1) * FF_CHUNK, :])

    def epilogue(r, rows):
        gate = _sigmoid(_dot(r.astype(BF16), wg_ref[...]))
        ple = gate * _dot(p_ref[rows, :].astype(BF16), wp_ref[...])
        o_ref[rows, :] = _layer_norm(r + ple, ln2g_ref[...], ln2b_ref[...])

    rows_a, rows_b = slice(0, half), slice(half, 2 * half)
    h1b_a, r_a = prologue(rows_a)
    h1b_b, r_b = prologue(rows_b)
    for c in range(n_ff):
        r_a = mlp_chunk(h1b_a, r_a, c)
        r_b = mlp_chunk(h1b_b, r_b, c)
    epilogue(r_a, rows_a)
    epilogue(r_b, rows_b)


def _channel_mixer(alpha, x2, mixed, p2, ln_g, ln_b, w_out, ln1_g, ln1_b, w_up, w_down, w_g, w_p, ln2_g, ln2_b):
    t = x2.shape[0]
    tm = TILE_FFN
    tok = lambda n: pl.BlockSpec((tm, n), lambda i: (i, 0))
    vec = _resident((1, D_MODEL))
    return pl.pallas_call(
        functools.partial(_ffn_body, alpha),
        grid=(t // tm,),
        in_specs=[tok(D_MODEL), tok(D_MODEL), tok(PLE_DIM), vec, vec,
                  _resident((D_MODEL, D_MODEL)), vec, vec,
                  _resident((D_MODEL, D_FF)), _resident((D_FF, D_MODEL)),
                  _resident((D_MODEL, D_MODEL)), _resident((PLE_DIM, D_MODEL)), vec, vec],
        out_specs=tok(D_MODEL),
        out_shape=jax.ShapeDtypeStruct((t, D_MODEL), F32),
        compiler_params=pltpu.CompilerParams(dimension_semantics=("arbitrary",),
                                             vmem_limit_bytes=VMEM_LIMIT),
        name="channel_mixer",
    )(x2, mixed, p2, ln_g, ln_b, w_out, ln1_g, ln1_b, w_up, w_down, w_g, w_p, ln2_g, ln2_b)


def kernel(x, p, ln_in_g, ln_in_b, w_in, pool_w, pool_scale, conv_w, a_log, dt_bias, o_norm_w, w_out,
           ln1_g, ln1_b, w_up, w_down, ple_gate_w, ple_proj_w, ln2_g, ln2_b):
    batch, seq, _ = x.shape
    depth = w_in.shape[0]
    assert depth == 1, "the fused input layer norm assumes a single layer"
    alpha = (2.0 * depth) ** 0.25
    tokens = batch * seq
    x2 = x.reshape(tokens, D_MODEL)
    row = lambda v: v.reshape(1, -1).astype(F32)

    i = 0
    o_beta = POOL_WIDTH + QKV_WIDTH + DN_WIDTH
    o_ga = o_beta + 2 * DN_HEADS
    w = w_in[i]
    w_main = w.astype(BF16)
    w_gates = jnp.concatenate(
        [w[:, o_ga:], w[:, o_beta:o_ga],
         jnp.zeros((D_MODEL, GATE_PAD - 2 * DN_HEADS), w.dtype)], axis=1).astype(BF16)
    pad = jnp.zeros((GATE_PAD - 2 * DN_HEADS,), F32)
    zeros_h = jnp.zeros((DN_HEADS,), F32)
    alog_vec = jnp.concatenate([zeros_h, a_log[i].astype(F32), pad]).reshape(1, GATE_PAD)
    dtb_vec = jnp.concatenate([zeros_h, dt_bias[i].astype(F32), pad]).reshape(1, GATE_PAD)
    yag, q, k, v, gm, bg = _in_projection(
        x2, row(ln_in_g), row(ln_in_b), w_main, w_gates, pool_w[i].astype(BF16), row(pool_scale[i]),
        conv_w[i].astype(F32), alog_vec, dtb_vec, row(jnp.tile(o_norm_w[i], DN_HEADS)), seq)

    mixed = _mixer(q, k, v, gm, yag, bg, batch, seq)

    out = _channel_mixer(alpha, x2, mixed, p[i].reshape(tokens, PLE_DIM), row(ln_in_g), row(ln_in_b),
                         w_out[i].astype(BF16), row(ln1_g[i]), row(ln1_b[i]),
                         w_up[i].astype(BF16), w_down[i].astype(BF16),
                         ple_gate_w[i].astype(BF16), ple_proj_w[i].astype(BF16),
                         row(ln2_g[i]), row(ln2_b[i]))
    return out.reshape(batch, seq, D_MODEL)
```

```python
import functools

import jax
import jax.numpy as jnp
from jax import lax
from jax.experimental import pallas as pl
from jax.experimental.pallas import tpu as pltpu

F32 = jnp.float32
BF16 = jnp.bfloat16

D_MODEL = 1024
POOL_WINDOWS = (2, 4, 8, 16)
N_POOL_GROUPS = 4
POOL_WIDTH = D_MODEL // 2
POOL_GROUP = POOL_WIDTH // N_POOL_GROUPS
POOL_OUT_GROUP = D_MODEL // N_POOL_GROUPS
DN_HEADS = 8
DN_HEAD_DIM = 128
DN_WIDTH = DN_HEADS * DN_HEAD_DIM
CONV_K = 4
D_FF = 4 * D_MODEL
PLE_DIM = 256
LN_EPS = 1e-5
RMS_EPS = 1e-6
L2_EPS = 1e-6
QKV_WIDTH = 3 * DN_WIDTH
LOG2E = 1.4426950408889634

LANES = 128
PAIR = 2 * LANES
N_PAIRS = DN_HEADS // 2
GATE_PAD = LANES
POOL_HIST = 16
CONV_HIST = 8

CHUNK = 128
TILE_PROJ = 256
TILE_MIX = 256
TILE_FFN = 512
FF_CHUNK = 1024
VMEM_LIMIT = 56 * 1024 * 1024

_O_QKV = POOL_WIDTH
_O_Z = _O_QKV + QKV_WIDTH
_O_GA = _O_Z + DN_WIDTH
_O_GB = _O_GA + D_MODEL
_O_BG = _O_GB + D_MODEL
_N_PROJ = _O_BG + GATE_PAD


def _layer_norm(x, g, b):
    mu = jnp.mean(x, axis=-1, keepdims=True)
    xc = x - mu
    var = jnp.mean(xc * xc, axis=-1, keepdims=True)
    return xc * lax.rsqrt(var + LN_EPS) * g + b


def _sigmoid(x):
    return 1.0 / (1.0 + jnp.exp2(x * (-LOG2E)))


def _softplus(x):
    return jnp.maximum(x, 0.0) + jnp.log1p(jnp.exp(-jnp.abs(x)))


def _dot(a, b):
    return jnp.dot(a, b, preferred_element_type=F32)


def _resident(shape):
    return pl.BlockSpec(shape, lambda *_: (0,) * len(shape), pipeline_mode=pl.Buffered(1))


def _proj_body(tiles_per_seq, x_ref, g_ref, b_ref, w_ref, wg_ref, poolw_ref, pscale_ref, convw_ref, alog_ref,
               dtb_ref, onw_ref, yag_ref, q_ref, k_ref, v_ref, gm_ref, bg_ref,
               ubuf_ref, qbuf_ref):
    tm = x_ref.shape[0]
    s = pl.program_id(0) % tiles_per_seq

    @pl.when(s == 0)
    def _():
        ubuf_ref[0:POOL_HIST, :] = jnp.zeros((POOL_HIST, POOL_WIDTH), F32)
        qbuf_ref[0:CONV_HIST, :] = jnp.zeros((CONV_HIST, QKV_WIDTH), F32)

    h = _layer_norm(x_ref[...], g_ref[...], b_ref[...]).astype(BF16)

    def proj(c0, n):
        if c0 < _O_GA:
            return _dot(h, w_ref[:, c0:c0 + n])
        return _dot(h, wg_ref[:, c0 - _O_GA:c0 - _O_GA + n])

    tpos = s * tm + lax.broadcasted_iota(jnp.int32, (tm, 1), 0)

    def pool_slice(c0):
        ubuf_ref[POOL_HIST:POOL_HIST + tm, c0:c0 + PAIR] = proj(c0, PAIR)

    def pool_group(gi):
        w = POOL_WINDOWS[gi]
        ue = ubuf_ref[0:POOL_HIST + tm, gi * POOL_GROUP:(gi + 1) * POOL_GROUP]
        win, span = ue, 1
        while span < w:
            win = win + pltpu.roll(win, span, axis=0)
            span *= 2
        cnt = jnp.minimum(tpos + 1, w).astype(F32)
        d = win[POOL_HIST:POOL_HIST + tm] / cnt - ue[POOL_HIST:POOL_HIST + tm]
        ocols = slice(gi * POOL_OUT_GROUP, (gi + 1) * POOL_OUT_GROUP)
        ya = _dot(d.astype(BF16), poolw_ref[gi]) * pscale_ref[:, ocols]
        ga = proj(_O_GA + gi * POOL_OUT_GROUP, POOL_OUT_GROUP)
        yag_ref[:, ocols] = _sigmoid(ga) * ya

    def gate_slice(c0, is_z):
        if is_z:
            pre = proj(_O_Z + c0, PAIR)
            gm_ref[:, c0:c0 + PAIR] = pre * _sigmoid(pre) * onw_ref[:, c0:c0 + PAIR]
        else:
            gm_ref[:, c0:c0 + PAIR] = gm_ref[:, c0:c0 + PAIR] * _sigmoid(proj(_O_GB + c0, PAIR))

    for c0 in range(0, POOL_WIDTH, PAIR):
        pool_slice(c0)
    for gi in range(N_POOL_GROUPS):
        pool_group(gi)

    side = ([functools.partial(gate_slice, c0, True) for c0 in range(0, DN_WIDTH, PAIR)]
            + [functools.partial(gate_slice, c0, False) for c0 in range(0, D_MODEL, PAIR)])
    n_groups = QKV_WIDTH // LANES
    n_slices = QKV_WIDTH // PAIR

    def issue_slice(sl):
        qbuf_ref[CONV_HIST:CONV_HIST + tm, sl * PAIR:(sl + 1) * PAIR] = proj(_O_QKV + sl * PAIR, PAIR)
        for unit in side[sl * len(side) // n_slices:(sl + 1) * len(side) // n_slices]:
            unit()

    for grp in range(n_groups):
        if grp % 2 == 0:
            issue_slice(grp // 2)
        cols = slice(grp * LANES, (grp + 1) * LANES)
        xe = qbuf_ref[0:CONV_HIST + tm, cols]
        acc = None
        for k in range(CONV_K):
            back = CONV_K - 1 - k
            xs = xe if back == 0 else pltpu.roll(xe, back, axis=0)
            term = xs[CONV_HIST:CONV_HIST + tm] * convw_ref[k:k + 1, cols]
            acc = term if acc is None else acc + term
        y = acc * _sigmoid(acc)
        hcols = slice((grp % DN_HEADS) * LANES, (grp % DN_HEADS + 1) * LANES)
        if grp < DN_HEADS:
            q_ref[:, hcols] = y * (lax.rsqrt(jnp.sum(y * y, axis=-1, keepdims=True) + L2_EPS)
                                   * (DN_HEAD_DIM ** -0.5))
        elif grp < 2 * DN_HEADS:
            k_ref[:, hcols] = y * lax.rsqrt(jnp.sum(y * y, axis=-1, keepdims=True) + L2_EPS)
        else:
            v_ref[:, hcols] = y
    qbuf_ref[0:CONV_HIST, :] = qbuf_ref[tm:tm + CONV_HIST, :]
    ubuf_ref[0:POOL_HIST, :] = ubuf_ref[tm:tm + POOL_HIST, :]

    ba = proj(_O_BG, GATE_PAD)
    lane = lax.broadcasted_iota(jnp.int32, (1, GATE_PAD), 1)
    is_decay = (lane >= DN_HEADS) & (lane < 2 * DN_HEADS)
    a_scale = jnp.where(is_decay, jnp.exp(alog_ref[...]), 0.0)
    g = -a_scale * _softplus(ba + dtb_ref[...])
    bg_ref[...] = jnp.where(lane < DN_HEADS, _sigmoid(ba), g)


def _in_projection(x2, ln_g, ln_b, w_main, w_gates, pool_w, pool_scale, conv_w, alog_vec, dtb_vec, onw_row, seq):
    t = x2.shape[0]
    tm = TILE_PROJ
    tok = lambda n: pl.BlockSpec((tm, n), lambda i: (i, 0))
    widths = (D_MODEL, DN_WIDTH, DN_WIDTH, DN_WIDTH, DN_WIDTH, GATE_PAD)
    return pl.pallas_call(
        functools.partial(_proj_body, seq // tm),
        grid=(t // tm,),
        in_specs=[tok(D_MODEL), _resident((1, D_MODEL)), _resident((1, D_MODEL)),
                  _resident(w_main.shape), _resident((D_MODEL, _N_PROJ - _O_GA)),
                  _resident((N_POOL_GROUPS, POOL_GROUP, POOL_OUT_GROUP)), _resident((1, D_MODEL)),
                  _resident((CONV_K, QKV_WIDTH)), _resident((1, GATE_PAD)), _resident((1, GATE_PAD)),
                  _resident((1, DN_WIDTH))],
        out_specs=[tok(n) for n in widths],
        out_shape=[jax.ShapeDtypeStruct((t, n), F32) for n in widths],
        scratch_shapes=[pltpu.VMEM((POOL_HIST + tm, POOL_WIDTH), F32),
                        pltpu.VMEM((CONV_HIST + tm, QKV_WIDTH), F32)],
        compiler_params=pltpu.CompilerParams(dimension_semantics=("arbitrary",),
                                             vmem_limit_bytes=VMEM_LIMIT),
        name="in_projection",
    )(x2, ln_g, ln_b, w_main, w_gates, pool_w, pool_scale, conv_w, alog_vec, dtb_vec, onw_row)


def _block_diag(a2):
    z = jnp.zeros((a2.shape[0], LANES), a2.dtype)
    return jnp.concatenate([jnp.concatenate([a2[:, :LANES], z], axis=1),
                            jnp.concatenate([z, a2[:, LANES:]], axis=1)], axis=0)


def _pair_dot(a2, b2):
    return _dot(a2, _block_diag(b2))


def _inverse_masks():
    r = lax.broadcasted_iota(jnp.int32, (CHUNK, PAIR), 0)
    c = lax.broadcasted_iota(jnp.int32, (CHUNK, PAIR), 1) % LANES
    strict = r > c
    eye = jnp.where(r == c, 1.0, 0.0).astype(F32)
    base = strict & ((r // 8) == (c // 8))
    levels = []
    b = 8
    while b < CHUNK:
        levels.append(strict & ((r // (2 * b)) == (c // (2 * b))) & ((r // b) != (c // b)))
        b *= 2
    return eye, base, levels


def _unit_lower_inverse(ms, masks):
    eye, base, levels = masks
    n = [jnp.where(base, m, 0.0) for m in ms]
    p = [eye - x for x in n]
    nb = [x.astype(BF16) for x in n]
    qb = [_pair_dot(x, x).astype(BF16) for x in nb]
    yield
    p = [x + _pair_dot(x.astype(BF16), y) for x, y in zip(p, qb)]
    qb = [_pair_dot(x, x).astype(BF16) for x in qb]
    yield
    t = [x + _pair_dot(x.astype(BF16), y) for x, y in zip(p, qb)]
    yield
    for lm in levels:
        cb = [jnp.where(lm, m, 0.0).astype(BF16) for m in ms]
        tb = [x.astype(BF16) for x in t]
        yb = [_pair_dot(c, x).astype(BF16) for c, x in zip(cb, tb)]
        yield
        t = [x - _pair_dot(xb, y) for x, xb, y in zip(t, tb, yb)]
        yield
    return t


def _interleave(gens):
    gens = list(gens)
    while gens:
        for g in list(gens):
            try:
                next(g)
            except StopIteration:
                gens.remove(g)


def _mixer_body(q_ref, k_ref, v_ref, gm_ref, yag_ref, bg_ref, out_ref, state_ref):
    n_seq, ts = out_ref.shape[0], out_ref.shape[1]
    n_chunks = ts // CHUNK
    units = [(b, ch) for b in range(n_seq) for ch in range(n_chunks)]

    @pl.when(pl.program_id(0) == 0)
    def _():
        state_ref[...] = jnp.zeros_like(state_ref)

    bg = [bg_ref[b] for b in range(n_seq)]
    b_hi = [x.astype(BF16) for x in bg]
    r1 = [x - y.astype(F32) for x, y in zip(bg, b_hi)]
    b_mid = [x.astype(BF16) for x in r1]
    b_lo = [(x - y.astype(F32)).astype(BF16) for x, y in zip(r1, b_mid)]
    ri = lax.broadcasted_iota(jnp.int32, (CHUNK, CHUNK), 0)
    ci = lax.broadcasted_iota(jnp.int32, (CHUNK, CHUNK), 1)
    tri = jnp.where(ri >= ci, 1.0, 0.0).astype(BF16)
    by_unit = lambda a: jnp.concatenate([a[b][ch * CHUNK:(ch + 1) * CHUNK] for b, ch in units], axis=1)
    gc_wide = (_dot(tri, by_unit(b_hi)) + _dot(tri, by_unit(b_mid))
               + _dot(tri, by_unit(b_lo)))
    gc_of = {u: gc_wide[:, i * LANES:(i + 1) * LANES] for i, u in enumerate(units)}
    gct_of = {u: g.T for u, g in gc_of.items()}

    r = lax.broadcasted_iota(jnp.int32, (CHUNK, PAIR), 0)
    c = lax.broadcasted_iota(jnp.int32, (CHUNK, PAIR), 1) % LANES
    incl = r >= c
    strict = r > c
    masks = _inverse_masks()

    def head_rms_scale(x2):
        parts = []
        for j in range(2):
            xh = x2[:, j * LANES:(j + 1) * LANES]
            ms = jnp.sum(xh * xh, axis=-1, keepdims=True) * (1.0 / DN_HEAD_DIM)
            parts.append(jnp.broadcast_to(lax.rsqrt(ms + RMS_EPS), (CHUNK, LANES)))
        return jnp.concatenate(parts, axis=1)

    def lanes_of(arr, rows, p, lane0):
        return jnp.concatenate(
            [jnp.broadcast_to(arr[rows, lane0 + 2 * p + j:lane0 + 2 * p + j + 1], (CHUNK, LANES))
             for j in range(2)], axis=1)

    rows_of = lambda ch: slice(ch * CHUNK, (ch + 1) * CHUNK)
    cols_of = lambda p: slice(p * PAIR, (p + 1) * PAIR)
    ready = {}

    def prepare():
        probs = [(b, ch, p) for b, ch in units for p in range(N_PAIRS)]
        k2 = [k_ref[b, rows_of(ch), cols_of(p)] for b, ch, p in probs]
        beta2 = [lanes_of(bg[b], rows_of(ch), p, 0) for b, ch, p in probs]
        gc2 = [lanes_of(gc_of[b, ch], slice(0, CHUNK), p, DN_HEADS) for b, ch, p in probs]
        gl2 = [lanes_of(gc_of[b, ch], slice(CHUNK - 1, CHUNK), p, DN_HEADS) for b, ch, p in probs]
        gcrow2 = [jnp.concatenate(
            [jnp.broadcast_to(gct_of[b, ch][DN_HEADS + 2 * p + j:DN_HEADS + 2 * p + j + 1, :],
                              (CHUNK, CHUNK)) for j in range(2)], axis=1) for b, ch, p in probs]
        decay2 = [jnp.where(incl, jnp.exp(a - b), 0.0) for a, b in zip(gc2, gcrow2)]
        egc2 = [jnp.exp(x) for x in gc2]
        kb2 = [x * y for x, y in zip(k2, beta2)]
        kbf = [x.astype(BF16) for x in k2]
        qbf = [q_ref[b, rows_of(ch), cols_of(p)].astype(BF16) for b, ch, p in probs]
        aq = [lax.dot_general(jnp.concatenate([x.astype(BF16), y], axis=0), _block_diag(z),
                              (((1,), (1,)), ((), ())), preferred_element_type=F32)
              for x, y, z in zip(kb2, qbf, kbf)]
        yield
        m2 = [jnp.where(strict, x[0:CHUNK] * d, 0.0) for x, d in zip(aq, decay2)]
        attn_b = [(x[CHUNK:2 * CHUNK] * d).astype(BF16) for x, d in zip(aq, decay2)]
        t2 = yield from _unit_lower_inverse(m2, masks)
        t_b = [x.astype(BF16) for x in t2]
        u2 = [_pair_dot(t, (v_ref[b, rows_of(ch), cols_of(p)] * bt).astype(BF16))
              for t, bt, (b, ch, p) in zip(t_b, beta2, probs)]
        w2 = [_pair_dot(t, (x * e).astype(BF16)) for t, x, e in zip(t_b, kb2, egc2)]
        yield
        wqg_b = [jnp.concatenate([w.astype(BF16), (q_ref[b, rows_of(ch), cols_of(p)] * e).astype(BF16)], axis=0)
                 for w, e, (b, ch, p) in zip(w2, egc2, probs)]
        kgt_b = [(x * jnp.exp(gl - gc)).T.astype(BF16)
                 for x, gl, gc in zip(k2, gl2, gc2)]
        egl2 = [jnp.exp(x) for x in gl2]
        for i, key in enumerate(probs):
            ready[key] = (u2[i], wqg_b[i], attn_b[i], kgt_b[i], egl2[i])
        yield

    lanes = [(b, p) for b in range(n_seq) for p in range(N_PAIRS)]
    states = [state_ref[b * N_PAIRS + p] for b, p in lanes]

    def recur():
        for ch in range(n_chunks):
            rows = rows_of(ch)
            u2, wqg_b, attn_b, kgt_b, egl2 = zip(*[ready[(b, ch, p)] for b, p in lanes])
            wq = [_pair_dot(a, st.astype(BF16)) for a, st in zip(wqg_b, states)]
            yield
            v_new_b = [(u - x[0:CHUNK]).astype(BF16) for u, x in zip(u2, wq)]
            o2 = [x[CHUNK:2 * CHUNK] + _pair_dot(a, vn) for x, a, vn in zip(wq, attn_b, v_new_b)]
            full = [_dot(kt, vn) for kt, vn in zip(kgt_b, v_new_b)]
            yield
            for i, f in enumerate(full):
                states[i] = states[i] * egl2[i] + jnp.concatenate(
                    [f[0:LANES, 0:LANES], f[LANES:PAIR, LANES:PAIR]], axis=1)
            yield
            for i, (b, p) in enumerate(lanes):
                cols = cols_of(p)
                yb = o2[i] * head_rms_scale(o2[i]) * gm_ref[b, rows, cols]
                out_ref[b, rows, cols] = (yag_ref[b, rows, cols] + yb).astype(out_ref.dtype)

    _interleave([prepare()])
    _interleave([recur()])

    for i, (b, p) in enumerate(lanes):
        state_ref[b * N_PAIRS + p] = states[i]


def _mixer(q, k, v, gm, yag, bg, batch, seq):
    ts = TILE_MIX
    tok = lambda n: pl.BlockSpec((batch, ts, n), lambda s: (0, s, 0))
    per_seq = lambda a: a.reshape(batch, seq, a.shape[-1])
    mixed = pl.pallas_call(
        _mixer_body,
        grid=(seq // ts,),
        in_specs=[tok(DN_WIDTH), tok(DN_WIDTH), tok(DN_WIDTH), tok(DN_WIDTH), tok(D_MODEL), tok(GATE_PAD)],
        out_specs=tok(D_MODEL),
        out_shape=jax.ShapeDtypeStruct((batch, seq, D_MODEL), BF16),
        scratch_shapes=[pltpu.VMEM((batch * N_PAIRS, DN_HEAD_DIM, PAIR), F32)],
        compiler_params=pltpu.CompilerParams(dimension_semantics=("arbitrary",),
                                             vmem_limit_bytes=VMEM_LIMIT),
        name="mixer",
    )(per_seq(q), per_seq(k), per_seq(v), per_seq(gm), per_seq(yag), per_seq(bg))
    return mixed.reshape(batch * seq, D_MODEL)


def _ffn_body(alpha, x_ref, mixed_ref, p_ref, lng_ref, lnb_ref, wout_ref, ln1g_ref, ln1b_ref,
              wup_ref, wdown_ref, wg_ref, wp_ref, ln2g_ref, ln2b_ref, o_ref):
    half = x_ref.shape[0] // 2
    n_ff = D_FF // FF_CHUNK

    def prologue(rows):
        h = _layer_norm(x_ref[rows, :], lng_ref[...], lnb_ref[...])
        t = alpha * h + _dot(mixed_ref[rows, :], wout_ref[...])
        h1 = _layer_norm(t, ln1g_ref[...], ln1b_ref[...])
        return h1.astype(BF16), alpha * h1

    def mlp_chunk(h1b, r, c):
        up = _dot(h1b, wup_ref[:, c * FF_CHUNK:(c + 1) * FF_CHUNK])
        act = jnp.square(jnp.maximum(up, 0.0)).astype(BF16)
        return r + _dot(act, wdown_ref[c * FF_CHUNK:(c + 1) * FF_CHUNK, :])

    def epilogue(r, rows):
        gate = _sigmoid(_dot(r.astype(BF16), wg_ref[...]))
        ple = gate * _dot(p_ref[rows, :].astype(BF16), wp_ref[...])
        o_ref[rows, :] = _layer_norm(r + ple, ln2g_ref[...], ln2b_ref[...])

    rows_a, rows_b = slice(0, half), slice(half, 2 * half)
    h1b_a, r_a = prologue(rows_a)
    h1b_b, r_b = prologue(rows_b)
    for c in range(n_ff):
        r_a = mlp_chunk(h1b_a, r_a, c)
        r_b = mlp_chunk(h1b_b, r_b, c)
    epilogue(r_a, rows_a)
    epilogue(r_b, rows_b)


def _channel_mixer(alpha, x2, mixed, p2, ln_g, ln_b, w_out, ln1_g, ln1_b, w_up, w_down, w_g, w_p, ln2_g, ln2_b):
    t = x2.shape[0]
    tm = TILE_FFN
    tok = lambda n: pl.BlockSpec((tm, n), lambda i: (i, 0))
    vec = _resident((1, D_MODEL))
    return pl.pallas_call(
        functools.partial(_ffn_body, alpha),
        grid=(t // tm,),
        in_specs=[tok(D_MODEL), tok(D_MODEL), tok(PLE_DIM), vec, vec,
                  _resident((D_MODEL, D_MODEL)), vec, vec,
                  _resident((D_MODEL, D_FF)), _resident((D_FF, D_MODEL)),
                  _resident((D_MODEL, D_MODEL)), _resident((PLE_DIM, D_MODEL)), vec, vec],
        out_specs=tok(D_MODEL),
        out_shape=jax.ShapeDtypeStruct((t, D_MODEL), F32),
        compiler_params=pltpu.CompilerParams(dimension_semantics=("arbitrary",),
                                             vmem_limit_bytes=VMEM_LIMIT),
        name="channel_mixer",
    )(x2, mixed, p2, ln_g, ln_b, w_out, ln1_g, ln1_b, w_up, w_down, w_g, w_p, ln2_g, ln2_b)


def kernel(x, p, ln_in_g, ln_in_b, w_in, pool_w, pool_scale, conv_w, a_log, dt_bias, o_norm_w, w_out,
           ln1_g, ln1_b, w_up, w_down, ple_gate_w, ple_proj_w, ln2_g, ln2_b):
    batch, seq, _ = x.shape
    depth = w_in.shape[0]
    assert depth == 1, "the fused input layer norm assumes a single layer"
    alpha = (2.0 * depth) ** 0.25
    tokens = batch * seq
    x2 = x.reshape(tokens, D_MODEL)
    row = lambda v: v.reshape(1, -1).astype(F32)

    i = 0
    o_beta = POOL_WIDTH + QKV_WIDTH + DN_WIDTH
    o_ga = o_beta + 2 * DN_HEADS
    w = w_in[i]
    w_main = jnp.pad(w, ((0, 0), (0, -w.shape[1] % LANES))).astype(BF16)
    w_gates = jnp.concatenate(
        [w[:, o_ga:], w[:, o_beta:o_ga],
         jnp.zeros((D_MODEL, GATE_PAD - 2 * DN_HEADS), w.dtype)], axis=1).astype(BF16)
    pad = jnp.zeros((GATE_PAD - 2 * DN_HEADS,), F32)
    zeros_h = jnp.zeros((DN_HEADS,), F32)
    alog_vec = jnp.concatenate([zeros_h, a_log[i].astype(F32), pad]).reshape(1, GATE_PAD)
    dtb_vec = jnp.concatenate([zeros_h, dt_bias[i].astype(F32), pad]).reshape(1, GATE_PAD)
    yag, q, k, v, gm, bg = _in_projection(
        x2, row(ln_in_g), row(ln_in_b), w_main, w_gates, pool_w[i].astype(BF16), row(pool_scale[i]),
        conv_w[i].astype(F32), alog_vec, dtb_vec, row(jnp.tile(o_norm_w[i], DN_HEADS)), seq)

    mixed = _mixer(q, k, v, gm, yag, bg, batch, seq)

    out = _channel_mixer(alpha, x2, mixed, p[i].reshape(tokens, PLE_DIM), row(ln_in_g), row(ln_in_b),
                         w_out[i].astype(BF16), row(ln1_g[i]), row(ln1_b[i]),
                         w_up[i].astype(BF16), w_down[i].astype(BF16),
                         ple_gate_w[i].astype(BF16), ple_proj_w[i].astype(BF16),
                         row(ln2_g[i]), row(ln2_b[i]))
    return out.reshape(batch, seq, D_MODEL)
```

```python
import functools

import jax
import jax.numpy as jnp
from jax import lax
from jax.experimental import pallas as pl
from jax.experimental.pallas import tpu as pltpu

F32 = jnp.float32
BF16 = jnp.bfloat16

D_MODEL = 1024
POOL_WINDOWS = (2, 4, 8, 16)
N_POOL_GROUPS = 4
POOL_WIDTH = D_MODEL // 2
POOL_GROUP = POOL_WIDTH // N_POOL_GROUPS
POOL_OUT_GROUP = D_MODEL // N_POOL_GROUPS
DN_HEADS = 8
DN_HEAD_DIM = 128
DN_WIDTH = DN_HEADS * DN_HEAD_DIM
CONV_K = 4
D_FF = 4 * D_MODEL
PLE_DIM = 256
LN_EPS = 1e-5
RMS_EPS = 1e-6
L2_EPS = 1e-6
QKV_WIDTH = 3 * DN_WIDTH
LOG2E = 1.4426950408889634

LANES = 128
PAIR = 2 * LANES
N_PAIRS = DN_HEADS // 2
GATE_PAD = LANES
POOL_HIST = 16
CONV_HIST = 8

CHUNK = 128
TILE_PROJ = 256
TILE_MIX = 256
TILE_FFN = 512
FF_CHUNK = 1024
VMEM_LIMIT = 56 * 1024 * 1024

_O_QKV = POOL_WIDTH
_O_Z = _O_QKV + QKV_WIDTH
_O_GA = _O_Z + DN_WIDTH
_O_GB = _O_GA + D_MODEL
_O_BG = _O_GB + D_MODEL
_N_PROJ = _O_BG + GATE_PAD


def _layer_norm(x, g, b):
    mu = jnp.mean(x, axis=-1, keepdims=True)
    xc = x - mu
    var = jnp.mean(xc * xc, axis=-1, keepdims=True)
    return xc * lax.rsqrt(var + LN_EPS) * g + b


def _sigmoid(x):
    return 1.0 / (1.0 + jnp.exp2(x * (-LOG2E)))


def _softplus(x):
    return jnp.maximum(x, 0.0) + jnp.log1p(jnp.exp(-jnp.abs(x)))


def _dot(a, b):
    return jnp.dot(a, b, preferred_element_type=F32)


def _resident(shape):
    return pl.BlockSpec(shape, lambda *_: (0,) * len(shape), pipeline_mode=pl.Buffered(1))


def _cast_body(w_ref, o_ref):
    o_ref[...] = w_ref[0].astype(o_ref.dtype)


def _cast_leading_columns(w3, layer, n_cols):
    rows = w3.shape[1]
    bw = 4 * LANES
    return pl.pallas_call(
        _cast_body,
        grid=(n_cols // bw,),
        in_specs=[pl.BlockSpec((1, rows, bw), lambda j: (layer, 0, j))],
        out_specs=pl.BlockSpec((rows, bw), lambda j: (0, j)),
        out_shape=jax.ShapeDtypeStruct((rows, n_cols), BF16),
        compiler_params=pltpu.CompilerParams(dimension_semantics=("arbitrary",)),
        name="weight_cast",
    )(w3)


def _proj_body(tiles_per_seq, x_ref, g_ref, b_ref, w_ref, wg_ref, poolw_ref, pscale_ref, convw_ref, alog_ref,
               dtb_ref, onw_ref, yag_ref, q_ref, k_ref, v_ref, gm_ref, bg_ref,
               ubuf_ref, qbuf_ref):
    tm = x_ref.shape[0]
    s = pl.program_id(0) % tiles_per_seq

    @pl.when(s == 0)
    def _():
        ubuf_ref[0:POOL_HIST, :] = jnp.zeros((POOL_HIST, POOL_WIDTH), F32)
        qbuf_ref[0:CONV_HIST, :] = jnp.zeros((CONV_HIST, QKV_WIDTH), F32)

    h = _layer_norm(x_ref[...], g_ref[...], b_ref[...]).astype(BF16)

    def proj(c0, n):
        if c0 < _O_GA:
            return _dot(h, w_ref[:, c0:c0 + n])
        return _dot(h, wg_ref[:, c0 - _O_GA:c0 - _O_GA + n])

    tpos = s * tm + lax.broadcasted_iota(jnp.int32, (tm, 1), 0)

    def pool_slice(c0):
        ubuf_ref[POOL_HIST:POOL_HIST + tm, c0:c0 + PAIR] = proj(c0, PAIR)

    def pool_group(gi):
        w = POOL_WINDOWS[gi]
        ue = ubuf_ref[0:POOL_HIST + tm, gi * POOL_GROUP:(gi + 1) * POOL_GROUP]
        win, span = ue, 1
        while span < w:
            win = win + pltpu.roll(win, span, axis=0)
            span *= 2
        cnt = jnp.minimum(tpos + 1, w).astype(F32)
        d = win[POOL_HIST:POOL_HIST + tm] / cnt - ue[POOL_HIST:POOL_HIST + tm]
        ocols = slice(gi * POOL_OUT_GROUP, (gi + 1) * POOL_OUT_GROUP)
        ya = _dot(d.astype(BF16), poolw_ref[gi]) * pscale_ref[:, ocols]
        ga = proj(_O_GA + gi * POOL_OUT_GROUP, POOL_OUT_GROUP)
        yag_ref[:, ocols] = _sigmoid(ga) * ya

    def gate_slice(c0, is_z):
        if is_z:
            pre = proj(_O_Z + c0, PAIR)
            gm_ref[:, c0:c0 + PAIR] = pre * _sigmoid(pre) * onw_ref[:, c0:c0 + PAIR]
        else:
            gm_ref[:, c0:c0 + PAIR] = gm_ref[:, c0:c0 + PAIR] * _sigmoid(proj(_O_GB + c0, PAIR))

    for c0 in range(0, POOL_WIDTH, PAIR):
        pool_slice(c0)
    for gi in range(N_POOL_GROUPS):
        pool_group(gi)

    side = ([functools.partial(gate_slice, c0, True) for c0 in range(0, DN_WIDTH, PAIR)]
            + [functools.partial(gate_slice, c0, False) for c0 in range(0, D_MODEL, PAIR)])
    n_groups = QKV_WIDTH // LANES
    n_slices = QKV_WIDTH // PAIR

    def issue_slice(sl):
        qbuf_ref[CONV_HIST:CONV_HIST + tm, sl * PAIR:(sl + 1) * PAIR] = proj(_O_QKV + sl * PAIR, PAIR)
        for unit in side[sl * len(side) // n_slices:(sl + 1) * len(side) // n_slices]:
            unit()

    for grp in range(n_groups):
        if grp % 2 == 0:
            issue_slice(grp // 2)
        cols = slice(grp * LANES, (grp + 1) * LANES)
        xe = qbuf_ref[0:CONV_HIST + tm, cols]
        acc = None
        for k in range(CONV_K):
            back = CONV_K - 1 - k
            xs = xe if back == 0 else pltpu.roll(xe, back, axis=0)
            term = xs[CONV_HIST:CONV_HIST + tm] * convw_ref[k:k + 1, cols]
            acc = term if acc is None else acc + term
        y = acc * _sigmoid(acc)
        hcols = slice((grp % DN_HEADS) * LANES, (grp % DN_HEADS + 1) * LANES)
        if grp < DN_HEADS:
            q_ref[:, hcols] = y * (lax.rsqrt(jnp.sum(y * y, axis=-1, keepdims=True) + L2_EPS)
                                   * (DN_HEAD_DIM ** -0.5))
        elif grp < 2 * DN_HEADS:
            k_ref[:, hcols] = y * lax.rsqrt(jnp.sum(y * y, axis=-1, keepdims=True) + L2_EPS)
        else:
            v_ref[:, hcols] = y
    qbuf_ref[0:CONV_HIST, :] = qbuf_ref[tm:tm + CONV_HIST, :]
    ubuf_ref[0:POOL_HIST, :] = ubuf_ref[tm:tm + POOL_HIST, :]

    ba = proj(_O_BG, GATE_PAD)
    lane = lax.broadcasted_iota(jnp.int32, (1, GATE_PAD), 1)
    is_decay = (lane >= DN_HEADS) & (lane < 2 * DN_HEADS)
    a_scale = jnp.where(is_decay, jnp.exp(alog_ref[...]), 0.0)
    g = -a_scale * _softplus(ba + dtb_ref[...])
    bg_ref[...] = jnp.where(lane < DN_HEADS, _sigmoid(ba), g)


def _in_projection(x2, ln_g, ln_b, w_main, w_gates, pool_w, pool_scale, conv_w, alog_vec, dtb_vec, onw_row, seq):
    t = x2.shape[0]
    tm = TILE_PROJ
    tok = lambda n: pl.BlockSpec((tm, n), lambda i: (i, 0))
    widths = (D_MODEL, DN_WIDTH, DN_WIDTH, DN_WIDTH, DN_WIDTH, GATE_PAD)
    return pl.pallas_call(
        functools.partial(_proj_body, seq // tm),
        grid=(t // tm,),
        in_specs=[tok(D_MODEL), _resident((1, D_MODEL)), _resident((1, D_MODEL)),
                  _resident(w_main.shape), _resident((D_MODEL, _N_PROJ - _O_GA)),
                  _resident((N_POOL_GROUPS, POOL_GROUP, POOL_OUT_GROUP)), _resident((1, D_MODEL)),
                  _resident((CONV_K, QKV_WIDTH)), _resident((1, GATE_PAD)), _resident((1, GATE_PAD)),
                  _resident((1, DN_WIDTH))],
        out_specs=[tok(n) for n in widths],
        out_shape=[jax.ShapeDtypeStruct((t, n), F32) for n in widths],
        scratch_shapes=[pltpu.VMEM((POOL_HIST + tm, POOL_WIDTH), F32),
                        pltpu.VMEM((CONV_HIST + tm, QKV_WIDTH), F32)],
        compiler_params=pltpu.CompilerParams(dimension_semantics=("arbitrary",),
                                             vmem_limit_bytes=VMEM_LIMIT),
        name="in_projection",
    )(x2, ln_g, ln_b, w_main, w_gates, pool_w, pool_scale, conv_w, alog_vec, dtb_vec, onw_row)


def _block_diag(a2):
    z = jnp.zeros((a2.shape[0], LANES), a2.dtype)
    return jnp.concatenate([jnp.concatenate([a2[:, :LANES], z], axis=1),
                            jnp.concatenate([z, a2[:, LANES:]], axis=1)], axis=0)


def _pair_dot(a2, b2):
    return _dot(a2, _block_diag(b2))


def _inverse_masks():
    r = lax.broadcasted_iota(jnp.int32, (CHUNK, PAIR), 0)
    c = lax.broadcasted_iota(jnp.int32, (CHUNK, PAIR), 1) % LANES
    strict = r > c
    eye = jnp.where(r == c, 1.0, 0.0).astype(F32)
    base = strict & ((r // 8) == (c // 8))
    levels = []
    b = 8
    while b < CHUNK:
        levels.append(strict & ((r // (2 * b)) == (c // (2 * b))) & ((r // b) != (c // b)))
        b *= 2
    return eye, base, levels


def _unit_lower_inverse(ms, masks):
    eye, base, levels = masks
    n = [jnp.where(base, m, 0.0) for m in ms]
    p = [eye - x for x in n]
    nb = [x.astype(BF16) for x in n]
    qb = [_pair_dot(x, x).astype(BF16) for x in nb]
    yield
    p = [x + _pair_dot(x.astype(BF16), y) for x, y in zip(p, qb)]
    qb = [_pair_dot(x, x).astype(BF16) for x in qb]
    yield
    t = [x + _pair_dot(x.astype(BF16), y) for x, y in zip(p, qb)]
    yield
    for lm in levels:
        cb = [jnp.where(lm, m, 0.0).astype(BF16) for m in ms]
        tb = [x.astype(BF16) for x in t]
        yb = [_pair_dot(c, x).astype(BF16) for c, x in zip(cb, tb)]
        yield
        t = [x - _pair_dot(xb, y) for x, xb, y in zip(t, tb, yb)]
        yield
    return t


def _interleave(gens):
    gens = list(gens)
    while gens:
        for g in list(gens):
            try:
                next(g)
            except StopIteration:
                gens.remove(g)


def _mixer_body(q_ref, k_ref, v_ref, gm_ref, yag_ref, bg_ref, out_ref, state_ref):
    n_seq, ts = out_ref.shape[0], out_ref.shape[1]
    n_chunks = ts // CHUNK
    units = [(b, ch) for b in range(n_seq) for ch in range(n_chunks)]

    @pl.when(pl.program_id(0) == 0)
    def _():
        state_ref[...] = jnp.zeros_like(state_ref)

    bg = [bg_ref[b] for b in range(n_seq)]
    b_hi = [x.astype(BF16) for x in bg]
    r1 = [x - y.astype(F32) for x, y in zip(bg, b_hi)]
    b_mid = [x.astype(BF16) for x in r1]
    b_lo = [(x - y.astype(F32)).astype(BF16) for x, y in zip(r1, b_mid)]
    ri = lax.broadcasted_iota(jnp.int32, (CHUNK, CHUNK), 0)
    ci = lax.broadcasted_iota(jnp.int32, (CHUNK, CHUNK), 1)
    tri = jnp.where(ri >= ci, 1.0, 0.0).astype(BF16)
    by_unit = lambda a: jnp.concatenate([a[b][ch * CHUNK:(ch + 1) * CHUNK] for b, ch in units], axis=1)
    gc_wide = (_dot(tri, by_unit(b_hi)) + _dot(tri, by_unit(b_mid))
               + _dot(tri, by_unit(b_lo)))
    gc_of = {u: gc_wide[:, i * LANES:(i + 1) * LANES] for i, u in enumerate(units)}
    gct_of = {u: g.T for u, g in gc_of.items()}

    r = lax.broadcasted_iota(jnp.int32, (CHUNK, PAIR), 0)
    c = lax.broadcasted_iota(jnp.int32, (CHUNK, PAIR), 1) % LANES
    incl = r >= c
    strict = r > c
    masks = _inverse_masks()

    def head_rms_scale(x2):
        parts = []
        for j in range(2):
            xh = x2[:, j * LANES:(j + 1) * LANES]
            ms = jnp.sum(xh * xh, axis=-1, keepdims=True) * (1.0 / DN_HEAD_DIM)
            parts.append(jnp.broadcast_to(lax.rsqrt(ms + RMS_EPS), (CHUNK, LANES)))
        return jnp.concatenate(parts, axis=1)

    def lanes_of(arr, rows, p, lane0):
        return jnp.concatenate(
            [jnp.broadcast_to(arr[rows, lane0 + 2 * p + j:lane0 + 2 * p + j + 1], (CHUNK, LANES))
             for j in range(2)], axis=1)

    rows_of = lambda ch: slice(ch * CHUNK, (ch + 1) * CHUNK)
    cols_of = lambda p: slice(p * PAIR, (p + 1) * PAIR)
    ready = {}

    def prepare():
        probs = [(b, ch, p) for b, ch in units for p in range(N_PAIRS)]
        k2 = [k_ref[b, rows_of(ch), cols_of(p)] for b, ch, p in probs]
        beta2 = [lanes_of(bg[b], rows_of(ch), p, 0) for b, ch, p in probs]
        gc2 = [lanes_of(gc_of[b, ch], slice(0, CHUNK), p, DN_HEADS) for b, ch, p in probs]
        gl2 = [lanes_of(gc_of[b, ch], slice(CHUNK - 1, CHUNK), p, DN_HEADS) for b, ch, p in probs]
        gcrow2 = [jnp.concatenate(
            [jnp.broadcast_to(gct_of[b, ch][DN_HEADS + 2 * p + j:DN_HEADS + 2 * p + j + 1, :],
                              (CHUNK, CHUNK)) for j in range(2)], axis=1) for b, ch, p in probs]
        decay2 = [jnp.where(incl, jnp.exp(a - b), 0.0) for a, b in zip(gc2, gcrow2)]
        egc2 = [jnp.exp(x) for x in gc2]
        kb2 = [x * y for x, y in zip(k2, beta2)]
        kbf = [x.astype(BF16) for x in k2]
        qbf = [q_ref[b, rows_of(ch), cols_of(p)].astype(BF16) for b, ch, p in probs]
        aq = [lax.dot_general(jnp.concatenate([x.astype(BF16), y], axis=0), _block_diag(z),
                              (((1,), (1,)), ((), ())), preferred_element_type=F32)
              for x, y, z in zip(kb2, qbf, kbf)]
        yield
        m2 = [jnp.where(strict, x[0:CHUNK] * d, 0.0) for x, d in zip(aq, decay2)]
        attn_b = [(x[CHUNK:2 * CHUNK] * d).astype(BF16) for x, d in zip(aq, decay2)]
        t2 = yield from _unit_lower_inverse(m2, masks)
        t_b = [x.astype(BF16) for x in t2]
        u2 = [_pair_dot(t, (v_ref[b, rows_of(ch), cols_of(p)] * bt).astype(BF16))
              for t, bt, (b, ch, p) in zip(t_b, beta2, probs)]
        w2 = [_pair_dot(t, (x * e).astype(BF16)) for t, x, e in zip(t_b, kb2, egc2)]
        yield
        wqg_b = [jnp.concatenate([w.astype(BF16), (q_ref[b, rows_of(ch), cols_of(p)] * e).astype(BF16)], axis=0)
                 for w, e, (b, ch, p) in zip(w2, egc2, probs)]
        kgt_b = [(x * jnp.exp(gl - gc)).T.astype(BF16)
                 for x, gl, gc in zip(k2, gl2, gc2)]
        egl2 = [jnp.exp(x) for x in gl2]
        for i, key in enumerate(probs):
            ready[key] = (u2[i], wqg_b[i], attn_b[i], kgt_b[i], egl2[i])
        yield

    lanes = [(b, p) for b in range(n_seq) for p in range(N_PAIRS)]
    states = [state_ref[b * N_PAIRS + p] for b, p in lanes]

    def recur():
        for ch in range(n_chunks):
            rows = rows_of(ch)
            u2, wqg_b, attn_b, kgt_b, egl2 = zip(*[ready[(b, ch, p)] for b, p in lanes])
            wq = [_pair_dot(a, st.astype(BF16)) for a, st in zip(wqg_b, states)]
            yield
            v_new_b = [(u - x[0:CHUNK]).astype(BF16) for u, x in zip(u2, wq)]
            o2 = [x[CHUNK:2 * CHUNK] + _pair_dot(a, vn) for x, a, vn in zip(wq, attn_b, v_new_b)]
            full = [_dot(kt, vn) for kt, vn in zip(kgt_b, v_new_b)]
            yield
            for i, f in enumerate(full):
                states[i] = states[i] * egl2[i] + jnp.concatenate(
                    [f[0:LANES, 0:LANES], f[LANES:PAIR, LANES:PAIR]], axis=1)
            yield
            for i, (b, p) in enumerate(lanes):
                cols = cols_of(p)
                yb = o2[i] * head_rms_scale(o2[i]) * gm_ref[b, rows, cols]
                out_ref[b, rows, cols] = (yag_ref[b, rows, cols] + yb).astype(out_ref.dtype)

    _interleave([prepare()])
    _interleave([recur()])

    for i, (b, p) in enumerate(lanes):
        state_ref[b * N_PAIRS + p] = states[i]


def _mixer(q, k, v, gm, yag, bg, batch, seq):
    ts = TILE_MIX
    tok = lambda n: pl.BlockSpec((batch, ts, n), lambda s: (0, s, 0))
    per_seq = lambda a: a.reshape(batch, seq, a.shape[-1])
    mixed = pl.pallas_call(
        _mixer_body,
        grid=(seq // ts,),
        in_specs=[tok(DN_WIDTH), tok(DN_WIDTH), tok(DN_WIDTH), tok(DN_WIDTH), tok(D_MODEL), tok(GATE_PAD)],
        out_specs=tok(D_MODEL),
        out_shape=jax.ShapeDtypeStruct((batch, seq, D_MODEL), BF16),
        scratch_shapes=[pltpu.VMEM((batch * N_PAIRS, DN_HEAD_DIM, PAIR), F32)],
        compiler_params=pltpu.CompilerParams(dimension_semantics=("arbitrary",),
                                             vmem_limit_bytes=VMEM_LIMIT),
        name="mixer",
    )(per_seq(q), per_seq(k), per_seq(v), per_seq(gm), per_seq(yag), per_seq(bg))
    return mixed.reshape(batch * seq, D_MODEL)


def _ffn_body(alpha, x_ref, mixed_ref, p_ref, lng_ref, lnb_ref, wout_ref, ln1g_ref, ln1b_ref,
              wup_ref, wdown_ref, wg_ref, wp_ref, ln2g_ref, ln2b_ref, o_ref):
    half = x_ref.shape[0] // 2
    n_ff = D_FF // FF_CHUNK

    def prologue(rows):
        h = _layer_norm(x_ref[rows, :], lng_ref[...], lnb_ref[...])
        t = alpha * h + _dot(mixed_ref[rows, :], wout_ref[...])
        h1 = _layer_norm(t, ln1g_ref[...], ln1b_ref[...])
        return h1.astype(BF16), alpha * h1

    def mlp_chunk(h1b, r, c):
        up = _dot(h1b, wup_ref[:, c * FF_CHUNK:(c + 1) * FF_CHUNK])
        act = jnp.square(jnp.maximum(up, 0.0)).astype(BF16)
        return r + _dot(act, wdown_ref[c * FF_CHUNK:(c + 1) * FF_CHUNK, :])

    def epilogue(r, rows):
        gate = _sigmoid(_dot(r.astype(BF16), wg_ref[...]))
        ple = gate * _dot(p_ref[rows, :].astype(BF16), wp_ref[...])
        o_ref[rows, :] = _layer_norm(r + ple, ln2g_ref[...], ln2b_ref[...])

    rows_a, rows_b = slice(0, half), slice(half, 2 * half)
    h1b_a, r_a = prologue(rows_a)
    h1b_b, r_b = prologue(rows_b)
    for c in range(n_ff):
        r_a = mlp_chunk(h1b_a, r_a, c)
        r_b = mlp_chunk(h1b_b, r_b, c)
    epilogue(r_a, rows_a)
    epilogue(r_b, rows_b)


def _channel_mixer(alpha, x2, mixed, p2, ln_g, ln_b, w_out, ln1_g, ln1_b, w_up, w_down, w_g, w_p, ln2_g, ln2_b):
    t = x2.shape[0]
    tm = TILE_FFN
    tok = lambda n: pl.BlockSpec((tm, n), lambda i: (i, 0))
    vec = _resident((1, D_MODEL))
    return pl.pallas_call(
        functools.partial(_ffn_body, alpha),
        grid=(t // tm,),
        in_specs=[tok(D_MODEL), tok(D_MODEL), tok(PLE_DIM), vec, vec,
                  _resident((D_MODEL, D_MODEL)), vec, vec,
                  _resident((D_MODEL, D_FF)), _resident((D_FF, D_MODEL)),
                  _resident((D_MODEL, D_MODEL)), _resident((PLE_DIM, D_MODEL)), vec, vec],
        out_specs=tok(D_MODEL),
        out_shape=jax.ShapeDtypeStruct((t, D_MODEL), F32),
        compiler_params=pltpu.CompilerParams(dimension_semantics=("arbitrary",),
                                             vmem_limit_bytes=VMEM_LIMIT),
        name="channel_mixer",
    )(x2, mixed, p2, ln_g, ln_b, w_out, ln1_g, ln1_b, w_up, w_down, w_g, w_p, ln2_g, ln2_b)


def kernel(x, p, ln_in_g, ln_in_b, w_in, pool_w, pool_scale, conv_w, a_log, dt_bias, o_norm_w, w_out,
           ln1_g, ln1_b, w_up, w_down, ple_gate_w, ple_proj_w, ln2_g, ln2_b):
    batch, seq, _ = x.shape
    depth = w_in.shape[0]
    assert depth == 1, "the fused input layer norm assumes a single layer"
    alpha = (2.0 * depth) ** 0.25
    tokens = batch * seq
    x2 = x.reshape(tokens, D_MODEL)
    row = lambda v: v.reshape(1, -1).astype(F32)

    i = 0
    o_beta = POOL_WIDTH + QKV_WIDTH + DN_WIDTH
    o_ga = o_beta + 2 * DN_HEADS
    w = w_in[i]
    w_main = _cast_leading_columns(w_in, i, o_beta)
    w_gates = jnp.concatenate(
        [w[:, o_ga:], w[:, o_beta:o_ga],
         jnp.zeros((D_MODEL, GATE_PAD - 2 * DN_HEADS), w.dtype)], axis=1).astype(BF16)
    pad = jnp.zeros((GATE_PAD - 2 * DN_HEADS,), F32)
    zeros_h = jnp.zeros((DN_HEADS,), F32)
    alog_vec = jnp.concatenate([zeros_h, a_log[i].astype(F32), pad]).reshape(1, GATE_PAD)
    dtb_vec = jnp.concatenate([zeros_h, dt_bias[i].astype(F32), pad]).reshape(1, GATE_PAD)
    yag, q, k, v, gm, bg = _in_projection(
        x2, row(ln_in_g), row(ln_in_b), w_main, w_gates, pool_w[i].astype(BF16), row(pool_scale[i]),
        conv_w[i].astype(F32), alog_vec, dtb_vec, row(jnp.tile(o_norm_w[i], DN_HEADS)), seq)

    mixed = _mixer(q, k, v, gm, yag, bg, batch, seq)

    out = _channel_mixer(alpha, x2, mixed, p[i].reshape(tokens, PLE_DIM), row(ln_in_g), row(ln_in_b),
                         w_out[i].astype(BF16), row(ln1_g[i]), row(ln1_b[i]),
                         w_up[i].astype(BF16), w_down[i].astype(BF16),
                         ple_gate_w[i].astype(BF16), ple_proj_w[i].astype(BF16),
                         row(ln2_g[i]), row(ln2_b[i]))
    return out.reshape(batch, seq, D_MODEL)
```

```python
import functools

import jax
import jax.numpy as jnp
from jax import lax
from jax.experimental import pallas as pl
from jax.experimental.pallas import tpu as pltpu

F32 = jnp.float32
BF16 = jnp.bfloat16

D_MODEL = 1024
POOL_WINDOWS = (2, 4, 8, 16)
N_POOL_GROUPS = 4
POOL_WIDTH = D_MODEL // 2
POOL_GROUP = POOL_WIDTH // N_POOL_GROUPS
POOL_OUT_GROUP = D_MODEL // N_POOL_GROUPS
DN_HEADS = 8
DN_HEAD_DIM = 128
DN_WIDTH = DN_HEADS * DN_HEAD_DIM
CONV_K = 4
D_FF = 4 * D_MODEL
PLE_DIM = 256
LN_EPS = 1e-5
RMS_EPS = 1e-6
L2_EPS = 1e-6
QKV_WIDTH = 3 * DN_WIDTH
LOG2E = 1.4426950408889634

LANES = 128
PAIR = 2 * LANES
N_PAIRS = DN_HEADS // 2
GATE_PAD = LANES
POOL_HIST = 16
CONV_HIST = 8

CHUNK = 128
TILE_PROJ = 256
TILE_MIX = 256
TILE_FFN = 512
FF_CHUNK = 1024
VMEM_LIMIT = 56 * 1024 * 1024

_O_QKV = POOL_WIDTH
_O_Z = _O_QKV + QKV_WIDTH
_O_GA = _O_Z + DN_WIDTH
_O_GB = _O_GA + D_MODEL
_O_BG = _O_GB + D_MODEL
_N_PROJ = _O_BG + GATE_PAD


def _layer_norm(x, g, b):
    mu = jnp.mean(x, axis=-1, keepdims=True)
    xc = x - mu
    var = jnp.mean(xc * xc, axis=-1, keepdims=True)
    return xc * lax.rsqrt(var + LN_EPS) * g + b


def _sigmoid(x):
    return 1.0 / (1.0 + jnp.exp2(x * (-LOG2E)))


def _softplus(x):
    return jnp.maximum(x, 0.0) + jnp.log1p(jnp.exp(-jnp.abs(x)))


def _dot(a, b):
    return jnp.dot(a, b, preferred_element_type=F32)


def _resident(shape):
    return pl.BlockSpec(shape, lambda *_: (0,) * len(shape), pipeline_mode=pl.Buffered(1))


def _proj_body(tiles_per_seq, x_ref, g_ref, b_ref, w_ref, wg_ref, poolw_ref, pscale_ref, convw_ref, alog_ref,
               dtb_ref, onw_ref, yag_ref, q_ref, k_ref, v_ref, gm_ref, bg_ref,
               ubuf_ref, qbuf_ref):
    tm = x_ref.shape[0]
    s = pl.program_id(0) % tiles_per_seq

    @pl.when(s == 0)
    def _():
        ubuf_ref[0:POOL_HIST, :] = jnp.zeros((POOL_HIST, POOL_WIDTH), F32)
        qbuf_ref[0:CONV_HIST, :] = jnp.zeros((CONV_HIST, QKV_WIDTH), F32)

    h = _layer_norm(x_ref[...], g_ref[...], b_ref[...]).astype(BF16)

    def proj(c0, n):
        if c0 < _O_GA:
            return _dot(h, w_ref[:, c0:c0 + n])
        return _dot(h, wg_ref[:, c0 - _O_GA:c0 - _O_GA + n])

    tpos = s * tm + lax.broadcasted_iota(jnp.int32, (tm, 1), 0)

    def pool_slice(c0):
        ubuf_ref[POOL_HIST:POOL_HIST + tm, c0:c0 + PAIR] = proj(c0, PAIR)

    def pool_group(gi):
        w = POOL_WINDOWS[gi]
        ue = ubuf_ref[0:POOL_HIST + tm, gi * POOL_GROUP:(gi + 1) * POOL_GROUP]
        win, span = ue, 1
        while span < w:
            win = win + pltpu.roll(win, span, axis=0)
            span *= 2
        cnt = jnp.minimum(tpos + 1, w).astype(F32)
        d = win[POOL_HIST:POOL_HIST + tm] / cnt - ue[POOL_HIST:POOL_HIST + tm]
        ocols = slice(gi * POOL_OUT_GROUP, (gi + 1) * POOL_OUT_GROUP)
        ya = _dot(d.astype(BF16), poolw_ref[gi]) * pscale_ref[:, ocols]
        ga = proj(_O_GA + gi * POOL_OUT_GROUP, POOL_OUT_GROUP)
        yag_ref[:, ocols] = _sigmoid(ga) * ya

    def gate_slice(c0, is_z):
        if is_z:
            pre = proj(_O_Z + c0, PAIR)
            gm_ref[:, c0:c0 + PAIR] = pre * _sigmoid(pre) * onw_ref[:, c0:c0 + PAIR]
        else:
            gm_ref[:, c0:c0 + PAIR] = gm_ref[:, c0:c0 + PAIR] * _sigmoid(proj(_O_GB + c0, PAIR))

    for c0 in range(0, POOL_WIDTH, PAIR):
        pool_slice(c0)
    for gi in range(N_POOL_GROUPS):
        pool_group(gi)

    side = ([functools.partial(gate_slice, c0, True) for c0 in range(0, DN_WIDTH, PAIR)]
            + [functools.partial(gate_slice, c0, False) for c0 in range(0, D_MODEL, PAIR)])
    n_groups = QKV_WIDTH // LANES
    n_slices = QKV_WIDTH // PAIR

    def issue_slice(sl):
        qbuf_ref[CONV_HIST:CONV_HIST + tm, sl * PAIR:(sl + 1) * PAIR] = proj(_O_QKV + sl * PAIR, PAIR)
        for unit in side[sl * len(side) // n_slices:(sl + 1) * len(side) // n_slices]:
            unit()

    for grp in range(n_groups):
        if grp % 2 == 0:
            issue_slice(grp // 2)
        cols = slice(grp * LANES, (grp + 1) * LANES)
        xe = qbuf_ref[0:CONV_HIST + tm, cols]
        acc = None
        for k in range(CONV_K):
            back = CONV_K - 1 - k
            xs = xe if back == 0 else pltpu.roll(xe, back, axis=0)
            term = xs[CONV_HIST:CONV_HIST + tm] * convw_ref[k:k + 1, cols]
            acc = term if acc is None else acc + term
        y = acc * _sigmoid(acc)
        hcols = slice((grp % DN_HEADS) * LANES, (grp % DN_HEADS + 1) * LANES)
        if grp < DN_HEADS:
            q_ref[:, hcols] = y * (lax.rsqrt(jnp.sum(y * y, axis=-1, keepdims=True) + L2_EPS)
                                   * (DN_HEAD_DIM ** -0.5))
        elif grp < 2 * DN_HEADS:
            k_ref[:, hcols] = y * lax.rsqrt(jnp.sum(y * y, axis=-1, keepdims=True) + L2_EPS)
        else:
            v_ref[:, hcols] = y
    qbuf_ref[0:CONV_HIST, :] = qbuf_ref[tm:tm + CONV_HIST, :]
    ubuf_ref[0:POOL_HIST, :] = ubuf_ref[tm:tm + POOL_HIST, :]

    ba = proj(_O_BG, GATE_PAD)
    lane = lax.broadcasted_iota(jnp.int32, (1, GATE_PAD), 1)
    is_decay = (lane >= DN_HEADS) & (lane < 2 * DN_HEADS)
    a_scale = jnp.where(is_decay, jnp.exp(alog_ref[...]), 0.0)
    g = -a_scale * _softplus(ba + dtb_ref[...])
    bg_ref[...] = jnp.where(lane < DN_HEADS, _sigmoid(ba), g)


def _in_projection(x2, ln_g, ln_b, w_main, w_gates, pool_w, pool_scale, conv_w, alog_vec, dtb_vec, onw_row, seq):
    t = x2.shape[0]
    tm = TILE_PROJ
    tok = lambda n: pl.BlockSpec((tm, n), lambda i: (i, 0))
    widths = (D_MODEL, DN_WIDTH, DN_WIDTH, DN_WIDTH, DN_WIDTH, GATE_PAD)
    return pl.pallas_call(
        functools.partial(_proj_body, seq // tm),
        grid=(t // tm,),
        in_specs=[tok(D_MODEL), _resident((1, D_MODEL)), _resident((1, D_MODEL)),
                  _resident(w_main.shape), _resident((D_MODEL, _N_PROJ - _O_GA)),
                  _resident((N_POOL_GROUPS, POOL_GROUP, POOL_OUT_GROUP)), _resident((1, D_MODEL)),
                  _resident((CONV_K, QKV_WIDTH)), _resident((1, GATE_PAD)), _resident((1, GATE_PAD)),
                  _resident((1, DN_WIDTH))],
        out_specs=[tok(n) for n in widths],
        out_shape=[jax.ShapeDtypeStruct((t, n), F32) for n in widths],
        scratch_shapes=[pltpu.VMEM((POOL_HIST + tm, POOL_WIDTH), F32),
                        pltpu.VMEM((CONV_HIST + tm, QKV_WIDTH), F32)],
        compiler_params=pltpu.CompilerParams(dimension_semantics=("arbitrary",),
                                             vmem_limit_bytes=VMEM_LIMIT),
        name="in_projection",
    )(x2, ln_g, ln_b, w_main, w_gates, pool_w, pool_scale, conv_w, alog_vec, dtb_vec, onw_row)


def _block_diag(a2):
    z = jnp.zeros((a2.shape[0], LANES), a2.dtype)
    return jnp.concatenate([jnp.concatenate([a2[:, :LANES], z], axis=1),
                            jnp.concatenate([z, a2[:, LANES:]], axis=1)], axis=0)


def _pair_dot(a2, b2):
    return _dot(a2, _block_diag(b2))


def _inverse_masks():
    r = lax.broadcasted_iota(jnp.int32, (CHUNK, PAIR), 0)
    c = lax.broadcasted_iota(jnp.int32, (CHUNK, PAIR), 1) % LANES
    strict = r > c
    eye = jnp.where(r == c, 1.0, 0.0).astype(F32)
    base = strict & ((r // 8) == (c // 8))
    levels = []
    b = 8
    while b < CHUNK:
        levels.append(strict & ((r // (2 * b)) == (c // (2 * b))) & ((r // b) != (c // b)))
        b *= 2
    return eye, base, levels


def _unit_lower_inverse(ms, masks):
    eye, base, levels = masks
    n = [jnp.where(base, m, 0.0) for m in ms]
    p = [eye - x for x in n]
    nb = [x.astype(BF16) for x in n]
    qb = [_pair_dot(x, x).astype(BF16) for x in nb]
    yield
    p = [x + _pair_dot(x.astype(BF16), y) for x, y in zip(p, qb)]
    qb = [_pair_dot(x, x).astype(BF16) for x in qb]
    yield
    t = [x + _pair_dot(x.astype(BF16), y) for x, y in zip(p, qb)]
    yield
    for lm in levels:
        cb = [jnp.where(lm, m, 0.0).astype(BF16) for m in ms]
        tb = [x.astype(BF16) for x in t]
        yb = [_pair_dot(c, x).astype(BF16) for c, x in zip(cb, tb)]
        yield
        t = [x - _pair_dot(xb, y) for x, xb, y in zip(t, tb, yb)]
        yield
    return t


def _interleave(gens):
    gens = list(gens)
    while gens:
        for g in list(gens):
            try:
                next(g)
            except StopIteration:
                gens.remove(g)


def _mixer_body(q_ref, k_ref, v_ref, gm_ref, yag_ref, bg_ref, out_ref, state_ref):
    n_seq, ts = out_ref.shape[0], out_ref.shape[1]
    n_chunks = ts // CHUNK
    units = [(b, ch) for b in range(n_seq) for ch in range(n_chunks)]

    @pl.when(pl.program_id(0) == 0)
    def _():
        state_ref[...] = jnp.zeros_like(state_ref)

    bg = [bg_ref[b] for b in range(n_seq)]
    b_hi = [x.astype(BF16) for x in bg]
    r1 = [x - y.astype(F32) for x, y in zip(bg, b_hi)]
    b_mid = [x.astype(BF16) for x in r1]
    b_lo = [(x - y.astype(F32)).astype(BF16) for x, y in zip(r1, b_mid)]
    ri = lax.broadcasted_iota(jnp.int32, (CHUNK, CHUNK), 0)
    ci = lax.broadcasted_iota(jnp.int32, (CHUNK, CHUNK), 1)
    tri = jnp.where(ri >= ci, 1.0, 0.0).astype(BF16)
    by_unit = lambda a: jnp.concatenate([a[b][ch * CHUNK:(ch + 1) * CHUNK] for b, ch in units], axis=1)
    gc_wide = (_dot(tri, by_unit(b_hi)) + _dot(tri, by_unit(b_mid))
               + _dot(tri, by_unit(b_lo)))
    gc_of = {u: gc_wide[:, i * LANES:(i + 1) * LANES] for i, u in enumerate(units)}
    gct_of = {u: g.T for u, g in gc_of.items()}

    r = lax.broadcasted_iota(jnp.int32, (CHUNK, PAIR), 0)
    c = lax.broadcasted_iota(jnp.int32, (CHUNK, PAIR), 1) % LANES
    incl = r >= c
    strict = r > c
    masks = _inverse_masks()

    def head_rms_scale(x2):
        parts = []
        for j in range(2):
            xh = x2[:, j * LANES:(j + 1) * LANES]
            ms = jnp.sum(xh * xh, axis=-1, keepdims=True) * (1.0 / DN_HEAD_DIM)
            parts.append(jnp.broadcast_to(lax.rsqrt(ms + RMS_EPS), (CHUNK, LANES)))
        return jnp.concatenate(parts, axis=1)

    def lanes_of(arr, rows, p, lane0):
        return jnp.concatenate(
            [jnp.broadcast_to(arr[rows, lane0 + 2 * p + j:lane0 + 2 * p + j + 1], (CHUNK, LANES))
             for j in range(2)], axis=1)

    rows_of = lambda ch: slice(ch * CHUNK, (ch + 1) * CHUNK)
    cols_of = lambda p: slice(p * PAIR, (p + 1) * PAIR)
    ready = {}

    def prepare():
        probs = [(b, ch, p) for b, ch in units for p in range(N_PAIRS)]
        k2 = [k_ref[b, rows_of(ch), cols_of(p)] for b, ch, p in probs]
        beta2 = [lanes_of(bg[b], rows_of(ch), p, 0) for b, ch, p in probs]
        gc2 = [lanes_of(gc_of[b, ch], slice(0, CHUNK), p, DN_HEADS) for b, ch, p in probs]
        gl2 = [lanes_of(gc_of[b, ch], slice(CHUNK - 1, CHUNK), p, DN_HEADS) for b, ch, p in probs]
        gcrow2 = [jnp.concatenate(
            [jnp.broadcast_to(gct_of[b, ch][DN_HEADS + 2 * p + j:DN_HEADS + 2 * p + j + 1, :],
                              (CHUNK, CHUNK)) for j in range(2)], axis=1) for b, ch, p in probs]
        decay2 = [jnp.where(incl, jnp.exp(a - b), 0.0) for a, b in zip(gc2, gcrow2)]
        egc2 = [jnp.exp(x) for x in gc2]
        kb2 = [x * y for x, y in zip(k2, beta2)]
        kbf = [x.astype(BF16) for x in k2]
        qbf = [q_ref[b, rows_of(ch), cols_of(p)].astype(BF16) for b, ch, p in probs]
        aq = [lax.dot_general(jnp.concatenate([x.astype(BF16), y], axis=0), _block_diag(z),
                              (((1,), (1,)), ((), ())), preferred_element_type=F32)
              for x, y, z in zip(kb2, qbf, kbf)]
        yield
        m2 = [jnp.where(strict, x[0:CHUNK] * d, 0.0) for x, d in zip(aq, decay2)]
        attn_b = [(x[CHUNK:2 * CHUNK] * d).astype(BF16) for x, d in zip(aq, decay2)]
        t2 = yield from _unit_lower_inverse(m2, masks)
        t_b = [x.astype(BF16) for x in t2]
        u2 = [_pair_dot(t, (v_ref[b, rows_of(ch), cols_of(p)] * bt).astype(BF16))
              for t, bt, (b, ch, p) in zip(t_b, beta2, probs)]
        w2 = [_pair_dot(t, (x * e).astype(BF16)) for t, x, e in zip(t_b, kb2, egc2)]
        yield
        wqg_b = [jnp.concatenate([w.astype(BF16), (q_ref[b, rows_of(ch), cols_of(p)] * e).astype(BF16)], axis=0)
                 for w, e, (b, ch, p) in zip(w2, egc2, probs)]
        kgt_b = [(x * jnp.exp(gl - gc)).T.astype(BF16)
                 for x, gl, gc in zip(k2, gl2, gc2)]
        egl2 = [jnp.exp(x) for x in gl2]
        for i, key in enumerate(probs):
            ready[key] = (u2[i], wqg_b[i], attn_b[i], kgt_b[i], egl2[i])
        yield

    lanes = [(b, p) for b in range(n_seq) for p in range(N_PAIRS)]
    states = [state_ref[b * N_PAIRS + p] for b, p in lanes]

    def recur():
        for ch in range(n_chunks):
            rows = rows_of(ch)
            u2, wqg_b, attn_b, kgt_b, egl2 = zip(*[ready[(b, ch, p)] for b, p in lanes])
            wq = [_pair_dot(a, st.astype(BF16)) for a, st in zip(wqg_b, states)]
            yield
            v_new_b = [(u - x[0:CHUNK]).astype(BF16) for u, x in zip(u2, wq)]
            o2 = [x[CHUNK:2 * CHUNK] + _pair_dot(a, vn) for x, a, vn in zip(wq, attn_b, v_new_b)]
            full = [_dot(kt, vn) for kt, vn in zip(kgt_b, v_new_b)]
            yield
            for i, f in enumerate(full):
                states[i] = states[i] * egl2[i] + jnp.concatenate(
                    [f[0:LANES, 0:LANES], f[LANES:PAIR, LANES:PAIR]], axis=1)
            yield
            for i, (b, p) in enumerate(lanes):
                cols = cols_of(p)
                yb = o2[i] * head_rms_scale(o2[i]) * gm_ref[b, rows, cols]
                out_ref[b, rows, cols] = (yag_ref[b, rows, cols] + yb).astype(out_ref.dtype)

    _interleave([prepare()])
    _interleave([recur()])

    for i, (b, p) in enumerate(lanes):
        state_ref[b * N_PAIRS + p] = states[i]


def _mixer(q, k, v, gm, yag, bg, batch, seq):
    ts = TILE_MIX
    tok = lambda n: pl.BlockSpec((batch, ts, n), lambda s: (0, s, 0))
    per_seq = lambda a: a.reshape(batch, seq, a.shape[-1])
    mixed = pl.pallas_call(
        _mixer_body,
        grid=(seq // ts,),
        in_specs=[tok(DN_WIDTH), tok(DN_WIDTH), tok(DN_WIDTH), tok(DN_WIDTH), tok(D_MODEL), tok(GATE_PAD)],
        out_specs=tok(D_MODEL),
        out_shape=jax.ShapeDtypeStruct((batch, seq, D_MODEL), BF16),
        scratch_shapes=[pltpu.VMEM((batch * N_PAIRS, DN_HEAD_DIM, PAIR), F32)],
        compiler_params=pltpu.CompilerParams(dimension_semantics=("arbitrary",),
                                             vmem_limit_bytes=VMEM_LIMIT),
        name="mixer",
    )(per_seq(q), per_seq(k), per_seq(v), per_seq(gm), per_seq(yag), per_seq(bg))
    return mixed.reshape(batch * seq, D_MODEL)


def _ffn_body(alpha, x_ref, mixed_ref, p_ref, lng_ref, lnb_ref, wout_ref, ln1g_ref, ln1b_ref,
              wup_ref, wdown_ref, wg_ref, wp_ref, ln2g_ref, ln2b_ref, o_ref):
    half = x_ref.shape[0] // 2
    n_ff = D_FF // FF_CHUNK

    def prologue(rows):
        h = _layer_norm(x_ref[rows, :], lng_ref[...], lnb_ref[...])
        t = alpha * h + _dot(mixed_ref[rows, :], wout_ref[...])
        h1 = _layer_norm(t, ln1g_ref[...], ln1b_ref[...])
        return h1.astype(BF16), alpha * h1

    def mlp_chunk(h1b, r, c):
        up = _dot(h1b, wup_ref[:, c * FF_CHUNK:(c + 1) * FF_CHUNK])
        act = jnp.square(jnp.maximum(up, 0.0)).astype(BF16)
        return r + _dot(act, wdown_ref[c * FF_CHUNK:(c + 1) * FF_CHUNK, :])

    def epilogue(r, rows):
        gate = _sigmoid(_dot(r.astype(BF16), wg_ref[...]))
        ple = gate * _dot(p_ref[rows, :].astype(BF16), wp_ref[...])
        o_ref[rows, :] = _layer_norm(r + ple, ln2g_ref[...], ln2b_ref[...])

    rows_a, rows_b = slice(0, half), slice(half, 2 * half)
    h1b_a, r_a = prologue(rows_a)
    h1b_b, r_b = prologue(rows_b)
    for c in range(n_ff):
        r_a = mlp_chunk(h1b_a, r_a, c)
        r_b = mlp_chunk(h1b_b, r_b, c)
    epilogue(r_a, rows_a)
    epilogue(r_b, rows_b)


def _channel_mixer(alpha, x2, mixed, p2, ln_g, ln_b, w_out, ln1_g, ln1_b, w_up, w_down, w_g, w_p, ln2_g, ln2_b):
    t = x2.shape[0]
    tm = TILE_FFN
    tok = lambda n: pl.BlockSpec((tm, n), lambda i: (i, 0))
    vec = _resident((1, D_MODEL))
    return pl.pallas_call(
        functools.partial(_ffn_body, alpha),
        grid=(t // tm,),
        in_specs=[tok(D_MODEL), tok(D_MODEL), tok(PLE_DIM), vec, vec,
                  _resident((D_MODEL, D_MODEL)), vec, vec,
                  _resident((D_MODEL, D_FF)), _resident((D_FF, D_MODEL)),
                  _resident((D_MODEL, D_MODEL)), _resident((PLE_DIM, D_MODEL)), vec, vec],
        out_specs=tok(D_MODEL),
        out_shape=jax.ShapeDtypeStruct((t, D_MODEL), F32),
        compiler_params=pltpu.CompilerParams(dimension_semantics=("arbitrary",),
                                             vmem_limit_bytes=VMEM_LIMIT),
        name="channel_mixer",
    )(x2, mixed, p2, ln_g, ln_b, w_out, ln1_g, ln1_b, w_up, w_down, w_g, w_p, ln2_g, ln2_b)


def kernel(x, p, ln_in_g, ln_in_b, w_in, pool_w, pool_scale, conv_w, a_log, dt_bias, o_norm_w, w_out,
           ln1_g, ln1_b, w_up, w_down, ple_gate_w, ple_proj_w, ln2_g, ln2_b):
    batch, seq, _ = x.shape
    depth = w_in.shape[0]
    assert depth == 1, "the fused input layer norm assumes a single layer"
    alpha = (2.0 * depth) ** 0.25
    tokens = batch * seq
    x2 = x.reshape(tokens, D_MODEL)
    row = lambda v: v.reshape(1, -1).astype(F32)

    i = 0
    o_beta = POOL_WIDTH + QKV_WIDTH + DN_WIDTH
    o_ga = o_beta + 2 * DN_HEADS
    w = w_in[i]
    w_main = w.astype(BF16)
    w_gates = jnp.concatenate(
        [w_main[:, o_ga:], w_main[:, o_beta:o_ga],
         jnp.zeros((D_MODEL, GATE_PAD - 2 * DN_HEADS), BF16)], axis=1)
    pad = jnp.zeros((GATE_PAD - 2 * DN_HEADS,), F32)
    zeros_h = jnp.zeros((DN_HEADS,), F32)
    alog_vec = jnp.concatenate([zeros_h, a_log[i].astype(F32), pad]).reshape(1, GATE_PAD)
    dtb_vec = jnp.concatenate([zeros_h, dt_bias[i].astype(F32), pad]).reshape(1, GATE_PAD)
    yag, q, k, v, gm, bg = _in_projection(
        x2, row(ln_in_g), row(ln_in_b), w_main, w_gates, pool_w[i].astype(BF16), row(pool_scale[i]),
        conv_w[i].astype(F32), alog_vec, dtb_vec, row(jnp.tile(o_norm_w[i], DN_HEADS)), seq)

    mixed = _mixer(q, k, v, gm, yag, bg, batch, seq)

    out = _channel_mixer(alpha, x2, mixed, p[i].reshape(tokens, PLE_DIM), row(ln_in_g), row(ln_in_b),
                         w_out[i].astype(BF16), row(ln1_g[i]), row(ln1_b[i]),
                         w_up[i].astype(BF16), w_down[i].astype(BF16),
                         ple_gate_w[i].astype(BF16), ple_proj_w[i].astype(BF16),
                         row(ln2_g[i]), row(ln2_b[i]))
    return out.reshape(batch, seq, D_MODEL)
```
